```python
import jax, jax.numpy as jnp
from jax import lax
import numpy as np

D_MODEL = 4096
BATCH = 2
SEQ = 8192
DEPTH = 2

HEAD_DIM = 64
N_Q_HEADS = 32
N_KV_HEADS = 8
Q_PER_KV = N_Q_HEADS // N_KV_HEADS
WINDOW = 128
BLOCK = 128
ROPE_THETA = 500000.0
ROPE_DIM = HEAD_DIM // 4
A_Q = N_Q_HEADS * HEAD_DIM
A_KV = N_KV_HEADS * HEAD_DIM
CONV_CH = D_MODEL // 2
CONV_WIDTH = 31
IN_EVEN = A_Q + 2 * A_KV + 2 * CONV_CH
MIX_EVEN = A_Q + CONV_CH
CHUNK = 128
GMLP_DIM = D_MODEL
GMLP_GROUPS = 16
GMLP_GROUP_DIM = GMLP_DIM // GMLP_GROUPS
N_GROUPS = 4
EXPERTS_PER_GROUP = 8
N_EXPERTS = N_GROUPS * EXPERTS_PER_GROUP
TOP_K = 2
D_EXPERT = 512

EPS = 1e-5
NEG_INF = -1e30

kernel_name = "hybrid_swa_conformer_gmlp_hmoe"


def rmsnorm(x, g):
    xf = x.astype(jnp.float32)
    y = xf * lax.rsqrt(jnp.mean(xf * xf, axis=-1, keepdims=True) + EPS)
    return (y * g.astype(jnp.float32)).astype(x.dtype)


def layernorm(x, g, b):
    xf = x.astype(jnp.float32)
    mu = jnp.mean(xf, axis=-1, keepdims=True)
    var = jnp.mean(jnp.square(xf - mu), axis=-1, keepdims=True)
    y = (xf - mu) * lax.rsqrt(var + EPS)
    return (y * g.astype(jnp.float32) + b.astype(jnp.float32)).astype(x.dtype)


def rope_tables(positions):
    inv_freq = ROPE_THETA ** (-jnp.arange(0, ROPE_DIM, 2, dtype=jnp.float32) / ROPE_DIM)
    ang = positions.astype(jnp.float32)[..., None] * inv_freq
    return jnp.cos(ang), jnp.sin(ang)


def apply_partial_rope(t, cos, sin):
    half = ROPE_DIM // 2
    tr = t[..., :ROPE_DIM].astype(jnp.float32)
    t1, t2 = tr[..., :half], tr[..., half:]
    c, s = cos[:, :, None, :], sin[:, :, None, :]
    rot = jnp.concatenate([t1 * c - t2 * s, t2 * c + t1 * s], axis=-1).astype(t.dtype)
    return jnp.concatenate([rot, t[..., ROPE_DIM:]], axis=-1)


def sliding_window_attention(q, k, v, sinks):
    B, S = q.shape[0], q.shape[1]
    nb = S // BLOCK
    qb = q.reshape(B, nb, BLOCK, N_KV_HEADS, Q_PER_KV, HEAD_DIM)
    kb = k.reshape(B, nb, BLOCK, N_KV_HEADS, HEAD_DIM)
    vb = v.reshape(B, nb, BLOCK, N_KV_HEADS, HEAD_DIM)
    pad = ((0, 0), (1, 0), (0, 0), (0, 0), (0, 0))
    kwin = jnp.concatenate([jnp.pad(kb, pad)[:, :-1], kb], axis=2)
    vwin = jnp.concatenate([jnp.pad(vb, pad)[:, :-1], vb], axis=2)
    scale = HEAD_DIM ** -0.5
    scores = jnp.einsum('bnqkgd,bnskd->bnkgqs', qb, kwin).astype(jnp.float32) * scale
    qi = jnp.arange(BLOCK)[:, None] + BLOCK
    kj = jnp.arange(2 * BLOCK)[None, :]
    local = (kj <= qi) & (qi - kj < WINDOW)
    not_first = jnp.arange(nb)[:, None, None] > 0
    valid = local[None] & (not_first | (kj[None] >= BLOCK))
    scores = jnp.where(valid[None, :, None, None], scores, NEG_INF)
    sink = jnp.broadcast_to(sinks.astype(jnp.float32).reshape(1, 1, N_KV_HEADS, Q_PER_KV, 1, 1),
                            scores.shape[:-1] + (1,))
    probs = jax.nn.softmax(jnp.concatenate([scores, sink], axis=-1), axis=-1)[..., :-1]
    out = jnp.einsum('bnkgqs,bnskd->bnqkgd', probs.astype(v.dtype), vwin)
    return out.reshape(B, S, A_Q)


def conformer_conv(glu_in, conv_w, conv_b, ln_g, ln_b):
    a, gate = jnp.split(glu_in, 2, axis=-1)
    h = a * jax.nn.sigmoid(gate)
    h = lax.conv_general_dilated(
        h, conv_w[:, None, :].astype(h.dtype), window_strides=(1,),
        padding=[(CONV_WIDTH - 1, 0)], dimension_numbers=('NWC', 'WIO', 'NWC'),
        feature_group_count=CONV_CH) + conv_b.astype(h.dtype)
    return jax.nn.silu(layernorm(h, ln_g, ln_b))


def even_mixer(x, cos, sin, norm, w_in, sinks, conv_w, conv_b, ln_g, ln_b, w_out):
    B, S, _ = x.shape
    h = rmsnorm(x, norm)
    proj = h @ w_in
    q, k, v, glu_in = jnp.split(proj, [A_Q, A_Q + A_KV, A_Q + 2 * A_KV], axis=-1)
    q = apply_partial_rope(q.reshape(B, S, N_Q_HEADS, HEAD_DIM), cos, sin)
    k = apply_partial_rope(k.reshape(B, S, N_KV_HEADS, HEAD_DIM), cos, sin)
    v = v.reshape(B, S, N_KV_HEADS, HEAD_DIM)
    attn = sliding_window_attention(q, k, v, sinks)
    conv = conformer_conv(glu_in, conv_w, conv_b, ln_g, ln_b)
    return jnp.concatenate([attn, conv], axis=-1) @ w_out


def odd_mixer(x, norm, w_in, ln_g, ln_b, w_s, b_s, w_out):
    B, S, _ = x.shape
    nc = S // CHUNK
    h = rmsnorm(x, norm)
    z = jax.nn.gelu(h @ w_in)
    u, v = jnp.split(z, 2, axis=-1)
    v = layernorm(v, ln_g, ln_b).reshape(B, nc, CHUNK, GMLP_GROUPS, GMLP_GROUP_DIM)
    causal = jnp.tril(jnp.ones((CHUNK, CHUNK), dtype=bool))
    ws = jnp.where(causal[None], w_s, 0.0).astype(v.dtype)
    sv = jnp.einsum('hts,bcshd->bcthd', ws, v) + b_s.T.astype(v.dtype)[None, None, :, :, None]
    return (u * sv.reshape(B, S, GMLP_DIM)) @ w_out


def hierarchical_moe(x, norm, w_rg, b_rg, w_re, b_re, w_gate, w_up, w_down):
    B, S, D = x.shape
    h = rmsnorm(x, norm).reshape(B * S, D)
    group_probs = jax.nn.softmax((h @ w_rg + b_rg).astype(jnp.float32), axis=-1)
    g_idx = jnp.argmax(group_probs, axis=-1)
    g_gate = jnp.take_along_axis(group_probs, g_idx[:, None], axis=-1)
    e_logits = (h @ w_re + b_re).astype(jnp.float32).reshape(-1, N_GROUPS, EXPERTS_PER_GROUP)
    sel = jnp.take_along_axis(e_logits, g_idx[:, None, None], axis=1)[:, 0]
    top_w, top_i = lax.top_k(jax.nn.softmax(sel, axis=-1), TOP_K)
    top_w = top_w / jnp.sum(top_w, axis=-1, keepdims=True) * g_gate
    expert_id = g_idx[:, None] * EXPERTS_PER_GROUP + top_i
    gates = jnp.sum(jax.nn.one_hot(expert_id, N_EXPERTS, dtype=jnp.float32) * top_w[..., None], axis=1)
    gates = gates.astype(h.dtype)
    hid = jax.nn.silu(jnp.einsum('td,edf->tef', h, w_gate)) * jnp.einsum('td,edf->tef', h, w_up)
    out = jnp.einsum('tef,efd->td', hid * gates[..., None], w_down)
    return out.reshape(B, S, D)


def setup_inputs(seed: int = 0) -> dict:
    key = jax.random.key(seed)
    ks = iter(jax.random.split(key, 48))

    def nrm(shape, scale):
        return jax.random.normal(next(ks), shape, jnp.float32) * scale

    def gain(n):
        return 1.0 + nrm((n,), 0.02)

    inp = {}
    inp['x'] = nrm((BATCH, SEQ, D_MODEL), 1.0)
    inp['positions'] = jnp.broadcast_to(jnp.arange(SEQ, dtype=jnp.int32), (BATCH, SEQ))
    inp['l0_norm_mix'] = gain(D_MODEL)
    inp['l0_w_in'] = nrm((D_MODEL, IN_EVEN), D_MODEL ** -0.5)
    inp['l0_sinks'] = nrm((N_Q_HEADS,), 0.5)
    inp['l0_conv_w'] = nrm((CONV_WIDTH, CONV_CH), CONV_WIDTH ** -0.5)
    inp['l0_conv_b'] = nrm((CONV_CH,), 0.02)
    inp['l0_ln_g'] = gain(CONV_CH)
    inp['l0_ln_b'] = nrm((CONV_CH,), 0.02)
    inp['l0_w_out'] = nrm((MIX_EVEN, D_MODEL), MIX_EVEN ** -0.5)
    inp['l0_norm_ffn'] = gain(D_MODEL)
    inp['l0_w_rg'] = nrm((D_MODEL, N_GROUPS), D_MODEL ** -0.5)
    inp['l0_b_rg'] = nrm((N_GROUPS,), 0.01)
    inp['l0_w_re'] = nrm((D_MODEL, N_EXPERTS), D_MODEL ** -0.5)
    inp['l0_b_re'] = nrm((N_EXPERTS,), 0.01)
    inp['l0_w_gate'] = nrm((N_EXPERTS, D_MODEL, D_EXPERT), D_MODEL ** -0.5)
    inp['l0_w_up'] = nrm((N_EXPERTS, D_MODEL, D_EXPERT), D_MODEL ** -0.5)
    inp['l0_w_down'] = nrm((N_EXPERTS, D_EXPERT, D_MODEL), D_EXPERT ** -0.5)
    inp['l1_norm_mix'] = gain(D_MODEL)
    inp['l1_w_in'] = nrm((D_MODEL, 2 * GMLP_DIM), D_MODEL ** -0.5)
    inp['l1_ln_g'] = gain(GMLP_DIM)
    inp['l1_ln_b'] = nrm((GMLP_DIM,), 0.02)
    inp['l1_w_s'] = nrm((GMLP_GROUPS, CHUNK, CHUNK), 0.5 * CHUNK ** -0.5)
    inp['l1_b_s'] = 1.0 + nrm((GMLP_GROUPS, CHUNK), 0.1)
    inp['l1_w_out'] = nrm((GMLP_DIM, D_MODEL), GMLP_DIM ** -0.5)
    inp['l1_norm_ffn'] = gain(D_MODEL)
    inp['l1_w_rg'] = nrm((D_MODEL, N_GROUPS), D_MODEL ** -0.5)
    inp['l1_b_rg'] = nrm((N_GROUPS,), 0.01)
    inp['l1_w_re'] = nrm((D_MODEL, N_EXPERTS), D_MODEL ** -0.5)
    inp['l1_b_re'] = nrm((N_EXPERTS,), 0.01)
    inp['l1_w_gate'] = nrm((N_EXPERTS, D_MODEL, D_EXPERT), D_MODEL ** -0.5)
    inp['l1_w_up'] = nrm((N_EXPERTS, D_MODEL, D_EXPERT), D_MODEL ** -0.5)
    inp['l1_w_down'] = nrm((N_EXPERTS, D_EXPERT, D_MODEL), D_EXPERT ** -0.5)
    inp['final_norm'] = gain(D_MODEL)
    return inp


def reference(x, positions,
              l0_norm_mix, l0_w_in, l0_sinks, l0_conv_w, l0_conv_b, l0_ln_g, l0_ln_b, l0_w_out,
              l0_norm_ffn, l0_w_rg, l0_b_rg, l0_w_re, l0_b_re, l0_w_gate, l0_w_up, l0_w_down,
              l1_norm_mix, l1_w_in, l1_ln_g, l1_ln_b, l1_w_s, l1_b_s, l1_w_out,
              l1_norm_ffn, l1_w_rg, l1_b_rg, l1_w_re, l1_b_re, l1_w_gate, l1_w_up, l1_w_down,
              final_norm):
    cos, sin = rope_tables(positions)
    mixer_params = (
        dict(norm=l0_norm_mix, w_in=l0_w_in, sinks=l0_sinks, conv_w=l0_conv_w, conv_b=l0_conv_b,
             ln_g=l0_ln_g, ln_b=l0_ln_b, w_out=l0_w_out),
        dict(norm=l1_norm_mix, w_in=l1_w_in, ln_g=l1_ln_g, ln_b=l1_ln_b, w_s=l1_w_s, b_s=l1_b_s,
             w_out=l1_w_out),
    )
    moe_params = (
        dict(norm=l0_norm_ffn, w_rg=l0_w_rg, b_rg=l0_b_rg, w_re=l0_w_re, b_re=l0_b_re,
             w_gate=l0_w_gate, w_up=l0_w_up, w_down=l0_w_down),
        dict(norm=l1_norm_ffn, w_rg=l1_w_rg, b_rg=l1_b_rg, w_re=l1_w_re, b_re=l1_b_re,
             w_gate=l1_w_gate, w_up=l1_w_up, w_down=l1_w_down),
    )
    for i in range(DEPTH):
        if i % 2 == 0:
            x = x + even_mixer(x, cos, sin, **mixer_params[i])
        else:
            x = x + odd_mixer(x, **mixer_params[i])
        x = x + hierarchical_moe(x, **moe_params[i])
    return rmsnorm(x, final_norm)
```

```python
import functools

import numpy as np
import jax
import jax.numpy as jnp
from jax import lax
from jax.experimental import pallas as pl
from jax.experimental.pallas import tpu as pltpu

D_MODEL = 4096
HEAD_DIM = 64
N_Q_HEADS = 32
N_KV_HEADS = 8
Q_PER_KV = N_Q_HEADS // N_KV_HEADS
BLOCK = 128
ROPE_THETA = 500000.0
ROPE_DIM = HEAD_DIM // 4
ROPE_HALF = ROPE_DIM // 2
A_Q = N_Q_HEADS * HEAD_DIM
A_KV = N_KV_HEADS * HEAD_DIM
CONV_CH = D_MODEL // 2
CONV_WIDTH = 31
CONV_HALO = 32
IN_EVEN = A_Q + 2 * A_KV + 2 * CONV_CH
CHUNK = 128
GMLP_DIM = D_MODEL
GMLP_GROUPS = 16
GMLP_GROUP_DIM = GMLP_DIM // GMLP_GROUPS
N_GROUPS = 4
EXPERTS_PER_GROUP = 8
N_EXPERTS = N_GROUPS * EXPERTS_PER_GROUP
D_EXPERT = 512
EPS = 1e-5
NEG_INF = -1e30

V7X_LANES = 128
V7X_SUBLANES = 8
V7X_VMEM_BYTES = 64 * 1024 * 1024
VMEM_LIMIT = 56 * 1024 * 1024

F32 = jnp.float32
BF16 = jnp.bfloat16


def _tile(n, want):
    t = min(n, want)
    while n % t:
        t //= 2
    return t


def _params(*sem):
    return pltpu.CompilerParams(dimension_semantics=sem, vmem_limit_bytes=VMEM_LIMIT)


def _sigmoid(x):
    return 1.0 / (1.0 + jnp.exp(-x))


def _gelu_tanh(x):
    c = np.float32(np.sqrt(2.0 / np.pi))
    return x * (0.5 * (1.0 + jnp.tanh(c * (x + 0.044715 * (x * x * x)))))


def _rmsnorm_rows(x, g):
    ms = jnp.mean(x * x, axis=-1, keepdims=True)
    return x * lax.rsqrt(ms + EPS) * g


def _rms_matmul_kernel(x_ref, g_ref, w_ref, o_ref, hn_ref, *, act):
    @pl.when(pl.program_id(1) == 0)
    def _():
        hn_ref[...] = _rmsnorm_rows(x_ref[...], g_ref[...]).astype(hn_ref.dtype)

    acc = jnp.dot(hn_ref[...], w_ref[...], preferred_element_type=F32)
    if act == "gelu":
        acc = _gelu_tanh(acc)
    o_ref[...] = acc.astype(o_ref.dtype)


def rms_matmul(x, g, w, *, act=None, out_dtype=BF16, name):
    t, d = x.shape
    n = w.shape[1]
    tm = _tile(t, 512)
    tn = _tile(n, 1024)
    return pl.pallas_call(
        functools.partial(_rms_matmul_kernel, act=act),
        out_shape=jax.ShapeDtypeStruct((t, n), out_dtype),
        grid=(t // tm, n // tn),
        in_specs=[
            pl.BlockSpec((tm, d), lambda i, j: (i, 0)),
            pl.BlockSpec((1, d), lambda i, j: (0, 0)),
            pl.BlockSpec((d, tn), lambda i, j: (0, j)),
        ],
        out_specs=pl.BlockSpec((tm, tn), lambda i, j: (i, j)),
        scratch_shapes=[pltpu.VMEM((tm, d), BF16)],
        compiler_params=_params("parallel", "arbitrary"),
        name=name,
    )(x, g.reshape(1, d), w)


def _rope_tables(pos_ref, invf_ref):
    ang = pos_ref[...].astype(F32) * invf_ref[...]
    c = jnp.cos(ang)
    s = jnp.sin(ang)
    lane = lax.broadcasted_iota(jnp.int32, ang.shape, 1) % HEAD_DIM
    s_lo = jnp.where(lane < ROPE_HALF, -s, 0.0)
    s_hi = jnp.where((lane >= ROPE_HALF) & (lane < ROPE_DIM), s, 0.0)
    return c, s_lo, s_hi


def _apply_rope(x, tables):
    c, s_lo, s_hi = tables
    w = x.shape[1]
    reps = w // c.shape[1]
    c, s_lo, s_hi = (jnp.concatenate([t] * reps, axis=1) for t in (c, s_lo, s_hi))
    return (x * c + pltpu.roll(x, w - ROPE_HALF, 1) * s_lo + pltpu.roll(x, ROPE_HALF, 1) * s_hi)


def _attn_kernel(sink_ref, pos_c_ref, pos_p_ref, invf_ref, q_ref, kc_ref, kp_ref, vc_ref, vp_ref,
                 o_ref, *, blocks_per_seq):
    i = pl.program_id(0)
    not_first = (i % blocks_per_seq) > 0
    tab_c = _rope_tables(pos_c_ref, invf_ref)
    tab_p = _rope_tables(pos_p_ref, invf_ref)
    scale = HEAD_DIM ** -0.5
    q = (_apply_rope(q_ref[...].astype(F32), tab_c) * scale).astype(BF16)
    kc = _apply_rope(kc_ref[...].astype(F32), tab_c).astype(BF16)
    kp = _apply_rope(kp_ref[...].astype(F32), tab_p).astype(BF16)
    vc = vc_ref[...]
    vp = vp_ref[...]

    qi = lax.broadcasted_iota(jnp.int32, (Q_PER_KV * BLOCK, 2 * BLOCK), 0) % BLOCK + BLOCK
    kj = lax.broadcasted_iota(jnp.int32, (Q_PER_KV * BLOCK, 2 * BLOCK), 1)
    valid = (kj <= qi) & (qi - kj < BLOCK) & (not_first | (kj >= BLOCK))

    outs = []
    for g in range(N_KV_HEADS):
        ksl = slice(g * HEAD_DIM, (g + 1) * HEAD_DIM)
        kg = jnp.concatenate([kp[:, ksl], kc[:, ksl]], axis=0)
        vg = jnp.concatenate([vp[:, ksl], vc[:, ksl]], axis=0)
        heads = [g * Q_PER_KV + h for h in range(Q_PER_KV)]
        qs = jnp.concatenate([q[:, h * HEAD_DIM:(h + 1) * HEAD_DIM] for h in heads], axis=0)
        s = lax.dot_general(qs, kg, (((1,), (1,)), ((), ())), preferred_element_type=F32)
        s = jnp.where(valid, s, NEG_INF)
        sink = jnp.concatenate([jnp.full((BLOCK, 1), sink_ref[h], F32) for h in heads], axis=0)
        m = jnp.maximum(jnp.max(s, axis=-1, keepdims=True), sink)
        p = jnp.exp(s - m)
        den = jnp.sum(p, axis=-1, keepdims=True) + jnp.exp(sink - m)
        o = jnp.dot(p.astype(BF16), vg, preferred_element_type=F32) / den
        outs.extend(o[h * BLOCK:(h + 1) * BLOCK] for h in range(Q_PER_KV))
    o_ref[...] = jnp.concatenate(outs, axis=1).astype(o_ref.dtype)


def attention(proj, positions, sinks, seq_len):
    t = proj.shape[0]
    nb = t // BLOCK
    invf = ROPE_THETA ** (-np.arange(0, ROPE_DIM, 2, dtype=np.float32) / ROPE_DIM)
    lane = np.arange(V7X_LANES) % HEAD_DIM
    invf_row = np.where(lane < ROPE_DIM, invf[lane % ROPE_HALF], 0.0).astype(np.float32)
    pos = positions.reshape(t, 1)
    kcol = A_Q // A_KV
    prev = lambda i, s: (jnp.maximum(i - 1, 0), 0)
    return pl.pallas_call(
        functools.partial(_attn_kernel, blocks_per_seq=seq_len // BLOCK),
        out_shape=jax.ShapeDtypeStruct((t, A_Q), BF16),
        grid_spec=pltpu.PrefetchScalarGridSpec(
            num_scalar_prefetch=1,
            grid=(nb,),
            in_specs=[
                pl.BlockSpec((BLOCK, 1), lambda i, s: (i, 0)),
                pl.BlockSpec((BLOCK, 1), prev),
                pl.BlockSpec((1, V7X_LANES), lambda i, s: (0, 0)),
                pl.BlockSpec((BLOCK, A_Q), lambda i, s: (i, 0)),
                pl.BlockSpec((BLOCK, A_KV), lambda i, s: (i, kcol)),
                pl.BlockSpec((BLOCK, A_KV), lambda i, s: (jnp.maximum(i - 1, 0), kcol)),
                pl.BlockSpec((BLOCK, A_KV), lambda i, s: (i, kcol + 1)),
                pl.BlockSpec((BLOCK, A_KV), lambda i, s: (jnp.maximum(i - 1, 0), kcol + 1)),
            ],
            out_specs=pl.BlockSpec((BLOCK, A_Q), lambda i, s: (i, 0)),
        ),
        compiler_params=_params("parallel"),
        name="swa_attention",
    )(sinks.astype(F32), pos, pos, jnp.asarray(invf_row).reshape(1, V7X_LANES),
      proj, proj, proj, proj, proj)


def _conv_kernel(a0_ref, a1_ref, g0_ref, g1_ref, w_ref, cb_ref, lg_ref, lb_ref, o_ref,
                 hbuf, ybuf, hsh, *, steps_per_seq):
    i = pl.program_id(0)
    ts = o_ref.shape[0]
    half = CONV_CH // 2

    @pl.when(i % steps_per_seq == 0)
    def _():
        hbuf[0:CONV_HALO, :] = jnp.zeros((CONV_HALO, CONV_CH), F32)

    @pl.when(i % steps_per_seq != 0)
    def _():
        hbuf[0:CONV_HALO, :] = hbuf[ts:ts + CONV_HALO, :]

    for c, (a_ref, g_ref) in enumerate(((a0_ref, g0_ref), (a1_ref, g1_ref))):
        hbuf[CONV_HALO:CONV_HALO + ts, c * half:(c + 1) * half] = (
            a_ref[...].astype(F32) * _sigmoid(g_ref[...].astype(F32)))

    rows = V7X_SUBLANES
    first_tap = CONV_HALO - (CONV_WIDTH - 1)
    h_all = hbuf[...]
    hsh[0] = h_all
    for s in range(1, rows):
        hsh[s] = pltpu.roll(h_all, ts + CONV_HALO - s, 0)

    def body(r, carry):
        base = pl.multiple_of(r * rows, rows)
        acc = jnp.broadcast_to(cb_ref[...], (rows, CONV_CH))
        for j in range(CONV_WIDTH):
            off = first_tap + j
            tap = hsh[off % rows, pl.ds(base + off // rows * rows, rows), :]
            acc = acc + tap * w_ref[j:j + 1, :]
        ybuf[pl.ds(base, rows), :] = acc
        return carry

    lax.fori_loop(0, ts // rows, body, 0)

    y = ybuf[...]
    mu = jnp.mean(y, axis=-1, keepdims=True)
    yc = y - mu
    var = jnp.mean(yc * yc, axis=-1, keepdims=True)
    z = yc * lax.rsqrt(var + EPS) * lg_ref[...] + lb_ref[...]
    o_ref[...] = (z * _sigmoid(z)).astype(o_ref.dtype)


def conformer_conv(proj, conv_w, conv_b, ln_g, ln_b, seq_len):
    t = proj.shape[0]
    ts = _tile(seq_len, 256)
    half = CONV_CH // 2
    col0 = (A_Q + 2 * A_KV) // half
    row = lambda v: v.reshape(1, CONV_CH).astype(F32)
    full = lambda shape: pl.BlockSpec(shape, lambda i: (0, 0))
    return pl.pallas_call(
        functools.partial(_conv_kernel, steps_per_seq=seq_len // ts),
        out_shape=jax.ShapeDtypeStruct((t, CONV_CH), BF16),
        grid=(t // ts,),
        in_specs=[
            pl.BlockSpec((ts, half), lambda i: (i, col0)),
            pl.BlockSpec((ts, half), lambda i: (i, col0 + 1)),
            pl.BlockSpec((ts, half), lambda i: (i, col0 + 2)),
            pl.BlockSpec((ts, half), lambda i: (i, col0 + 3)),
            full((CONV_WIDTH, CONV_CH)),
            full((1, CONV_CH)), full((1, CONV_CH)), full((1, CONV_CH)),
        ],
        out_specs=pl.BlockSpec((ts, CONV_CH), lambda i: (i, 0)),
        scratch_shapes=[pltpu.VMEM((ts + CONV_HALO, CONV_CH), F32), pltpu.VMEM((ts, CONV_CH), F32),
                        pltpu.VMEM((V7X_SUBLANES, ts + CONV_HALO, CONV_CH), F32)],
        compiler_params=_params("arbitrary"),
        name="conformer_conv",
    )(proj, proj, proj, proj, conv_w.astype(F32), row(conv_b), row(ln_g), row(ln_b))


def _matmul_res_kernel(*refs, n_a):
    a_refs, w_refs = refs[:n_a], refs[n_a:2 * n_a]
    r_ref, o_ref = refs[2 * n_a], refs[2 * n_a + 1]
    acc = r_ref[...]
    for a_ref, w_ref in zip(a_refs, w_refs):
        acc = acc + jnp.dot(a_ref[...], w_ref[...], preferred_element_type=F32)
    o_ref[...] = acc


def matmul_res(a_list, w_list, res, *, name):
    t, n = res.shape
    tm = _tile(t, 512)
    tn = _tile(n, 1024)
    n_a = len(a_list)
    in_specs = [pl.BlockSpec((tm, a.shape[1]), lambda i, j: (i, 0)) for a in a_list]
    in_specs += [pl.BlockSpec((w.shape[0], tn), lambda i, j: (0, j)) for w in w_list]
    in_specs += [pl.BlockSpec((tm, tn), lambda i, j: (i, j))]
    return pl.pallas_call(
        functools.partial(_matmul_res_kernel, n_a=n_a),
        out_shape=jax.ShapeDtypeStruct((t, n), F32),
        grid=(t // tm, n // tn),
        in_specs=in_specs,
        out_specs=pl.BlockSpec((tm, tn), lambda i, j: (i, j)),
        compiler_params=_params("parallel", "arbitrary"),
        name=name,
    )(*a_list, *w_list, res)


def _sgu_kernel(u_ref, v_ref, lg_ref, lb_ref, ws_ref, bs_ref, o_ref):
    rows = o_ref.shape[0]
    v = v_ref[...].astype(F32)
    mu = jnp.mean(v, axis=-1, keepdims=True)
    vc = v - mu
    var = jnp.mean(vc * vc, axis=-1, keepdims=True)
    vn = (vc * lax.rsqrt(var + EPS) * lg_ref[...] + lb_ref[...]).astype(BF16)
    ti = lax.broadcasted_iota(jnp.int32, (CHUNK, CHUNK), 0)
    si = lax.broadcasted_iota(jnp.int32, (CHUNK, CHUNK), 1)
    causal = si <= ti
    for c in range(rows // CHUNK):
        rsl = slice(c * CHUNK, (c + 1) * CHUNK)
        for g in range(GMLP_GROUPS):
            csl = slice(g * GMLP_GROUP_DIM, (g + 1) * GMLP_GROUP_DIM)
            ws = jnp.where(causal, ws_ref[g], 0.0).astype(BF16)
            sv = jnp.dot(ws, vn[rsl, csl], preferred_element_type=F32) + bs_ref[:, g:g + 1]
            o_ref[rsl, csl] = (u_ref[rsl, csl].astype(F32) * sv).astype(o_ref.dtype)


def spatial_gating(z, ln_g, ln_b, w_s, b_s):
    t = z.shape[0]
    rows = CHUNK
    row = lambda v: v.reshape(1, GMLP_DIM).astype(F32)
    return pl.pallas_call(
        _sgu_kernel,
        out_shape=jax.ShapeDtypeStruct((t, GMLP_DIM), BF16),
        grid=(t // rows,),
        in_specs=[
            pl.BlockSpec((rows, GMLP_DIM), lambda i: (i, 0)),
            pl.BlockSpec((rows, GMLP_DIM), lambda i: (i, 1)),
            pl.BlockSpec((1, GMLP_DIM), lambda i: (0, 0)),
            pl.BlockSpec((1, GMLP_DIM), lambda i: (0, 0)),
            pl.BlockSpec((GMLP_GROUPS, CHUNK, CHUNK), lambda i: (0, 0, 0)),
            pl.BlockSpec((CHUNK, GMLP_GROUPS), lambda i: (0, 0)),
        ],
        out_specs=pl.BlockSpec((rows, GMLP_DIM), lambda i: (i, 0)),
        compiler_params=_params("parallel"),
        name="spatial_gating",
    )(z, z, row(ln_g), row(ln_b), w_s.astype(F32), b_s.T.astype(F32))


ROUTER_LANES = V7X_LANES


def _first_argmax(x, lane, width):
    m = jnp.max(x, axis=-1, keepdims=True)
    idx = jnp.min(jnp.where(x == m, lane, width), axis=-1, keepdims=True)
    return m, idx


def _router_kernel(x_ref, g_ref, w_ref, b_ref, hn_ref, eid_ref, gate_ref, rank_ref, cnt_ref, carry):
    i = pl.program_id(0)
    tm = x_ref.shape[0]

    @pl.when(i == 0)
    def _():
        carry[...] = jnp.zeros_like(carry)

    h = _rmsnorm_rows(x_ref[...], g_ref[...])
    hn_ref[...] = h
    logits = jnp.dot(h, w_ref[...], preferred_element_type=F32,
                     precision=lax.Precision.HIGHEST) + b_ref[...]
    lane = lax.broadcasted_iota(jnp.int32, logits.shape, 1)
    gl = jnp.where(lane < N_GROUPS, logits, NEG_INF)
    gmax, g_idx = _first_argmax(gl, lane, ROUTER_LANES)
    g_gate = 1.0 / jnp.sum(jnp.exp(gl - gmax), axis=-1, keepdims=True)
    e_lane = lane - N_GROUPS
    in_group = (e_lane >= g_idx * EXPERTS_PER_GROUP) & (e_lane < (g_idx + 1) * EXPERTS_PER_GROUP)
    el = jnp.where(in_group, logits, NEG_INF)
    l1, i1 = _first_argmax(el, lane, ROUTER_LANES)
    el2 = jnp.where(lane == i1, NEG_INF, el)
    l2, i2 = _first_argmax(el2, lane, ROUTER_LANES)
    r = jnp.exp(l2 - l1)
    w1 = g_gate / (1.0 + r)
    w2 = g_gate * r / (1.0 + r)
    e1 = i1 - N_GROUPS
    e2 = i2 - N_GROUPS
    hot1 = lane == e1
    hot2 = lane == e2
    hot = (hot1 | hot2).astype(BF16)
    ri = lax.broadcasted_iota(jnp.int32, (tm, tm), 0)
    ci = lax.broadcasted_iota(jnp.int32, (tm, tm), 1)
    before = (ci < ri).astype(BF16)
    seen = jnp.dot(before, hot, preferred_element_type=F32) + carry[...]
    r1 = jnp.sum(jnp.where(hot1, seen, 0.0), axis=-1, keepdims=True)
    r2 = jnp.sum(jnp.where(hot2, seen, 0.0), axis=-1, keepdims=True)
    carry[...] = carry[...] + jnp.sum(hot.astype(F32), axis=0, keepdims=True)
    eid_ref[...] = jnp.concatenate([e1, e2], axis=1)
    gate_ref[...] = jnp.concatenate([w1, w2], axis=1)
    rank_ref[...] = jnp.concatenate([r1, r2], axis=1).astype(jnp.int32)
    cnt_ref[...] = carry[...].astype(jnp.int32)


def moe_router(x, norm, w_rg, b_rg, w_re, b_re):
    t, d = x.shape
    tm = _tile(t, 256)
    pad = ROUTER_LANES - N_GROUPS - N_EXPERTS
    w = jnp.concatenate([w_rg, w_re, jnp.zeros((d, pad), F32)], axis=1)
    b = jnp.concatenate([b_rg, b_re, jnp.zeros((pad,), F32)]).reshape(1, ROUTER_LANES)
    pair = lambda dt: jax.ShapeDtypeStruct((t, 2), dt)
    pair_spec = pl.BlockSpec((tm, 2), lambda i: (i, 0))
    return pl.pallas_call(
        _router_kernel,
        out_shape=(jax.ShapeDtypeStruct((t, d), F32), pair(jnp.int32), pair(F32), pair(jnp.int32),
                   jax.ShapeDtypeStruct((1, ROUTER_LANES), jnp.int32)),
        grid=(t // tm,),
        in_specs=[
            pl.BlockSpec((tm, d), lambda i: (i, 0)),
            pl.BlockSpec((1, d), lambda i: (0, 0)),
            pl.BlockSpec((d, ROUTER_LANES), lambda i: (0, 0)),
            pl.BlockSpec((1, ROUTER_LANES), lambda i: (0, 0)),
        ],
        out_specs=(pl.BlockSpec((tm, d), lambda i: (i, 0)), pair_spec, pair_spec, pair_spec,
                   pl.BlockSpec((1, ROUTER_LANES), lambda i: (0, 0))),
        scratch_shapes=[pltpu.VMEM((1, ROUTER_LANES), F32)],
        compiler_params=_params("arbitrary"),
        name="moe_router",
    )(x, norm.reshape(1, d), w, b)


def _row_copy(src_hbm, dst, src_row, dst_row, sem):
    return pltpu.make_async_copy(src_hbm.at[pl.ds(src_row, 1)], dst.at[pl.ds(dst_row, 1)], sem)


def _dispatch_kernel(dest_ref, hn_hbm, init_hbm, hs_hbm, sem, *, tokens_per_step):
    del init_hbm
    base = pl.program_id(0) * tokens_per_step

    def start(k, carry):
        tok = base + k
        _row_copy(hn_hbm, hs_hbm, tok, dest_ref[2 * tok], sem).start()
        _row_copy(hn_hbm, hs_hbm, tok, dest_ref[2 * tok + 1], sem).start()
        return carry

    def wait(k, carry):
        _row_copy(hn_hbm, hs_hbm, 0, 0, sem).wait()
        _row_copy(hn_hbm, hs_hbm, 0, 0, sem).wait()
        return carry

    lax.fori_loop(0, tokens_per_step, start, 0)
    lax.fori_loop(0, tokens_per_step, wait, 0)


def moe_dispatch(hn, dest, n_slots):
    t, d = hn.shape
    step = _tile(t, 256)
    init = jnp.zeros((n_slots, d), hn.dtype)
    return pl.pallas_call(
        functools.partial(_dispatch_kernel, tokens_per_step=step),
        out_shape=jax.ShapeDtypeStruct((n_slots, d), hn.dtype),
        grid_spec=pltpu.PrefetchScalarGridSpec(
            num_scalar_prefetch=1,
            grid=(t // step,),
            in_specs=[pl.BlockSpec(memory_space=pl.ANY), pl.BlockSpec(memory_space=pl.ANY)],
            out_specs=pl.BlockSpec(memory_space=pl.ANY),
            scratch_shapes=[pltpu.SemaphoreType.DMA],
        ),
        input_output_aliases={2: 0},
        compiler_params=_params("arbitrary"),
        name="moe_dispatch",
    )(dest, hn, init)


def _experts_kernel(te_ref, na_ref, hs_ref, wg_ref, wu_ref, wd_ref, ys_ref):
    @pl.when(pl.program_id(0) < na_ref[0])
    def _():
        h = hs_ref[...].astype(BF16)
        g = jnp.dot(h, wg_ref[...], preferred_element_type=F32)
        u = jnp.dot(h, wu_ref[...], preferred_element_type=F32)
        act = (g * _sigmoid(g) * u).astype(BF16)
        ys_ref[...] = jnp.dot(act, wd_ref[...], preferred_element_type=F32)

    @pl.when(pl.program_id(0) >= na_ref[0])
    def _():
        ys_ref[...] = jnp.zeros_like(ys_ref)


def moe_experts(hs, tile_expert, n_active, w_gate, w_up, w_down, tm):
    n_slots, d = hs.shape
    n_tiles = n_slots // tm
    row_map = lambda i, te, na: (jnp.minimum(i, na[0] - 1), 0)
    w_map = lambda i, te, na: (te[i], 0, 0)
    return pl.pallas_call(
        _experts_kernel,
        out_shape=jax.ShapeDtypeStruct((n_slots, d), F32),
        grid_spec=pltpu.PrefetchScalarGridSpec(
            num_scalar_prefetch=2,
            grid=(n_tiles,),
            in_specs=[
                pl.BlockSpec((tm, d), row_map),
                pl.BlockSpec((None, d, D_EXPERT), w_map),
                pl.BlockSpec((None, d, D_EXPERT), w_map),
                pl.BlockSpec((None, D_EXPERT, d), w_map),
            ],
            out_specs=pl.BlockSpec((tm, d), lambda i, te, na: (i, 0)),
        ),
        compiler_params=_params("arbitrary"),
        name="moe_experts",
    )(tile_expert, n_active, hs, w_gate, w_up, w_down)


def _combine_kernel(dest_ref, x_ref, gate_ref, ys_hbm, *rest, final_norm):
    if final_norm:
        fg_ref, o_ref, ybuf, sem = rest
    else:
        o_ref, ybuf, sem = rest
    tc = x_ref.shape[0]
    base = pl.program_id(0) * tc

    def start(k, carry):
        tok = base + k
        _row_copy(ys_hbm, ybuf.at[0], dest_ref[2 * tok], k, sem).start()
        _row_copy(ys_hbm, ybuf.at[1], dest_ref[2 * tok + 1], k, sem).start()
        return carry

    def wait(k, carry):
        _row_copy(ys_hbm, ybuf.at[0], 0, 0, sem).wait()
        _row_copy(ys_hbm, ybuf.at[1], 0, 0, sem).wait()
        return carry

    lax.fori_loop(0, tc, start, 0)
    lax.fori_loop(0, tc, wait, 0)
    gate = gate_ref[...]
    y = x_ref[...] + gate[:, 0:1] * ybuf[0] + gate[:, 1:2] * ybuf[1]
    if final_norm:
        y = _rmsnorm_rows(y, fg_ref[...])
    o_ref[...] = y


def moe_combine(x, gates, dest, ys, final_g=None):
    t, d = x.shape
    tc = _tile(t, 256)
    final_norm = final_g is not None
    in_specs = [
        pl.BlockSpec((tc, d), lambda i, dst: (i, 0)),
        pl.BlockSpec((tc, 2), lambda i, dst: (i, 0)),
        pl.BlockSpec(memory_space=pl.ANY),
    ]
    args = [x, gates, ys]
    if final_norm:
        in_specs.append(pl.BlockSpec((1, d), lambda i, dst: (0, 0)))
        args.append(final_g.reshape(1, d))
    return pl.pallas_call(
        functools.partial(_combine_kernel, final_norm=final_norm),
        out_shape=jax.ShapeDtypeStruct((t, d), F32),
        grid_spec=pltpu.PrefetchScalarGridSpec(
            num_scalar_prefetch=1,
            grid=(t // tc,),
            in_specs=in_specs,
            out_specs=pl.BlockSpec((tc, d), lambda i, dst: (i, 0)),
            scratch_shapes=[pltpu.VMEM((2, tc, d), F32), pltpu.SemaphoreType.DMA],
        ),
        compiler_params=_params("arbitrary"),
        name="moe_combine",
    )(dest, *args)


def hierarchical_moe(x, norm, w_rg, b_rg, w_re, b_re, w_gate, w_up, w_down, final_g=None):
    t, d = x.shape
    tm = _tile(t, 256)
    n_slots = 2 * t + N_EXPERTS * tm
    hn, eid, gates, rank, counts = moe_router(x, norm, w_rg, b_rg, w_re, b_re)
    counts = counts[0, :N_EXPERTS]
    padded = (counts + tm - 1) // tm * tm
    ends = jnp.cumsum(padded)
    dest = ((ends - padded)[eid] + rank).reshape(2 * t).astype(jnp.int32)
    n_active = (ends[-1] // tm).astype(jnp.int32)
    tile_row = jnp.minimum(jnp.arange(n_slots // tm, dtype=jnp.int32), n_active - 1) * tm
    tile_expert = jnp.sum(tile_row[:, None] >= ends[None, :], axis=1).astype(jnp.int32)
    hs = moe_dispatch(hn, dest, n_slots)
    ys = moe_experts(hs, tile_expert, n_active.reshape(1), w_gate, w_up, w_down, tm)
    return moe_combine(x, gates, dest, ys, final_g)


def kernel(x, positions, l0_norm_mix, l0_w_in, l0_sinks, l0_conv_w, l0_conv_b, l0_ln_g, l0_ln_b, l0_w_out, l0_norm_ffn, l0_w_rg, l0_b_rg, l0_w_re, l0_b_re, l0_w_gate, l0_w_up, l0_w_down, l1_norm_mix, l1_w_in, l1_ln_g, l1_ln_b, l1_w_s, l1_b_s, l1_w_out, l1_norm_ffn, l1_w_rg, l1_b_rg, l1_w_re, l1_b_re, l1_w_gate, l1_w_up, l1_w_down, final_norm):
    b, s, d = x.shape
    t = b * s
    x = x.reshape(t, d)
    bf = lambda w: w.astype(BF16)

    proj = rms_matmul(x, l0_norm_mix, bf(l0_w_in), name="l0_in_proj")
    attn = attention(proj, positions, l0_sinks, s)
    conv = conformer_conv(proj, l0_conv_w, l0_conv_b, l0_ln_g, l0_ln_b, s)
    w_out0 = bf(l0_w_out)
    x = matmul_res([attn, conv], [w_out0[:A_Q], w_out0[A_Q:]], x, name="l0_out_proj")
    x = hierarchical_moe(x, l0_norm_ffn, l0_w_rg, l0_b_rg, l0_w_re, l0_b_re,
                         bf(l0_w_gate), bf(l0_w_up), bf(l0_w_down))
    z = rms_matmul(x, l1_norm_mix, bf(l1_w_in), act="gelu", name="l1_in_proj")
    gated = spatial_gating(z, l1_ln_g, l1_ln_b, l1_w_s, l1_b_s)
    x = matmul_res([gated], [bf(l1_w_out)], x, name="l1_out_proj")
    x = hierarchical_moe(x, l1_norm_ffn, l1_w_rg, l1_b_rg, l1_w_re, l1_b_re,
                         bf(l1_w_gate), bf(l1_w_up), bf(l1_w_down), final_g=final_norm)
    return x.reshape(b, s, d)
```

```python
import functools

import numpy as np
import jax
import jax.numpy as jnp
from jax import lax
from jax.experimental import pallas as pl
from jax.experimental.pallas import tpu as pltpu

D_MODEL = 4096
HEAD_DIM = 64
N_Q_HEADS = 32
N_KV_HEADS = 8
Q_PER_KV = N_Q_HEADS // N_KV_HEADS
BLOCK = 128
ROPE_THETA = 500000.0
ROPE_DIM = HEAD_DIM // 4
ROPE_HALF = ROPE_DIM // 2
A_Q = N_Q_HEADS * HEAD_DIM
A_KV = N_KV_HEADS * HEAD_DIM
CONV_CH = D_MODEL // 2
CONV_WIDTH = 31
CONV_HALO = 32
IN_EVEN = A_Q + 2 * A_KV + 2 * CONV_CH
CHUNK = 128
GMLP_DIM = D_MODEL
GMLP_GROUPS = 16
GMLP_GROUP_DIM = GMLP_DIM // GMLP_GROUPS
N_GROUPS = 4
EXPERTS_PER_GROUP = 8
N_EXPERTS = N_GROUPS * EXPERTS_PER_GROUP
D_EXPERT = 512
EPS = 1e-5
NEG_INF = -1e30

V7X_LANES = 128
V7X_SUBLANES = 8
V7X_VMEM_BYTES = 64 * 1024 * 1024
VMEM_LIMIT = 56 * 1024 * 1024

F32 = jnp.float32
BF16 = jnp.bfloat16


def _tile(n, want):
    t = min(n, want)
    while n % t:
        t //= 2
    return t


def _params(*sem):
    return pltpu.CompilerParams(dimension_semantics=sem, vmem_limit_bytes=VMEM_LIMIT)


def _sigmoid(x):
    return 1.0 / (1.0 + jnp.exp(-x))


def _gelu_tanh(x):
    c = np.float32(np.sqrt(2.0 / np.pi))
    return x * (0.5 * (1.0 + jnp.tanh(c * (x + 0.044715 * (x * x * x)))))


def _rmsnorm_rows(x, g):
    ms = jnp.mean(x * x, axis=-1, keepdims=True)
    return x * lax.rsqrt(ms + EPS) * g


def _rms_matmul_kernel(x_ref, g_ref, w_ref, o_ref, hn_ref, *, act):
    @pl.when(pl.program_id(1) == 0)
    def _():
        hn_ref[...] = _rmsnorm_rows(x_ref[...], g_ref[...]).astype(hn_ref.dtype)

    acc = jnp.dot(hn_ref[...], w_ref[...], preferred_element_type=F32)
    if act == "gelu":
        acc = _gelu_tanh(acc)
    o_ref[...] = acc.astype(o_ref.dtype)


def rms_matmul(x, g, w, *, act=None, out_dtype=BF16, name):
    t, d = x.shape
    n = w.shape[1]
    tm = _tile(t, 512)
    tn = _tile(n, 1024)
    return pl.pallas_call(
        functools.partial(_rms_matmul_kernel, act=act),
        out_shape=jax.ShapeDtypeStruct((t, n), out_dtype),
        grid=(t // tm, n // tn),
        in_specs=[
            pl.BlockSpec((tm, d), lambda i, j: (i, 0)),
            pl.BlockSpec((1, d), lambda i, j: (0, 0)),
            pl.BlockSpec((d, tn), lambda i, j: (0, j)),
        ],
        out_specs=pl.BlockSpec((tm, tn), lambda i, j: (i, j)),
        scratch_shapes=[pltpu.VMEM((tm, d), BF16)],
        compiler_params=_params("parallel", "arbitrary"),
        name=name,
    )(x, g.reshape(1, d), w)


def _rope_tables(pos_ref, invf_ref):
    ang = pos_ref[...].astype(F32) * invf_ref[...]
    c = jnp.cos(ang)
    s = jnp.sin(ang)
    lane = lax.broadcasted_iota(jnp.int32, ang.shape, 1) % HEAD_DIM
    s_lo = jnp.where(lane < ROPE_HALF, -s, 0.0)
    s_hi = jnp.where((lane >= ROPE_HALF) & (lane < ROPE_DIM), s, 0.0)
    return c, s_lo, s_hi


def _apply_rope(x, tables):
    c, s_lo, s_hi = tables
    w = x.shape[1]
    reps = w // c.shape[1]
    c, s_lo, s_hi = (jnp.concatenate([t] * reps, axis=1) for t in (c, s_lo, s_hi))
    return (x * c + pltpu.roll(x, w - ROPE_HALF, 1) * s_lo + pltpu.roll(x, ROPE_HALF, 1) * s_hi)


def _attn_kernel(sink_ref, pos_c_ref, pos_p_ref, invf_ref, q_ref, kc_ref, kp_ref, vc_ref, vp_ref,
                 o_ref, *, blocks_per_seq):
    i = pl.program_id(0)
    not_first = (i % blocks_per_seq) > 0
    tab_c = _rope_tables(pos_c_ref, invf_ref)
    tab_p = _rope_tables(pos_p_ref, invf_ref)
    scale = HEAD_DIM ** -0.5
    q = (_apply_rope(q_ref[...].astype(F32), tab_c) * scale).astype(BF16)
    kc = _apply_rope(kc_ref[...].astype(F32), tab_c).astype(BF16)
    kp = _apply_rope(kp_ref[...].astype(F32), tab_p).astype(BF16)
    vc = vc_ref[...]
    vp = vp_ref[...]

    qi = lax.broadcasted_iota(jnp.int32, (Q_PER_KV * BLOCK, 2 * BLOCK), 0) % BLOCK + BLOCK
    kj = lax.broadcasted_iota(jnp.int32, (Q_PER_KV * BLOCK, 2 * BLOCK), 1)
    valid = (kj <= qi) & (qi - kj < BLOCK) & (not_first | (kj >= BLOCK))

    outs = []
    for g in range(N_KV_HEADS):
        ksl = slice(g * HEAD_DIM, (g + 1) * HEAD_DIM)
        kg = jnp.concatenate([kp[:, ksl], kc[:, ksl]], axis=0)
        vg = jnp.concatenate([vp[:, ksl], vc[:, ksl]], axis=0)
        heads = [g * Q_PER_KV + h for h in range(Q_PER_KV)]
        qs = jnp.concatenate([q[:, h * HEAD_DIM:(h + 1) * HEAD_DIM] for h in heads], axis=0)
        s = lax.dot_general(qs, kg, (((1,), (1,)), ((), ())), preferred_element_type=F32)
        s = jnp.where(valid, s, NEG_INF)
        sink = jnp.concatenate([jnp.full((BLOCK, 1), sink_ref[h], F32) for h in heads], axis=0)
        m = jnp.maximum(jnp.max(s, axis=-1, keepdims=True), sink)
        p = jnp.exp(s - m)
        den = jnp.sum(p, axis=-1, keepdims=True) + jnp.exp(sink - m)
        o = jnp.dot(p.astype(BF16), vg, preferred_element_type=F32) / den
        outs.extend(o[h * BLOCK:(h + 1) * BLOCK] for h in range(Q_PER_KV))
    o_ref[...] = jnp.concatenate(outs, axis=1).astype(o_ref.dtype)


def attention(proj, positions, sinks, seq_len):
    t = proj.shape[0]
    nb = t // BLOCK
    invf = ROPE_THETA ** (-np.arange(0, ROPE_DIM, 2, dtype=np.float32) / ROPE_DIM)
    lane = np.arange(V7X_LANES) % HEAD_DIM
    invf_row = np.where(lane < ROPE_DIM, invf[lane % ROPE_HALF], 0.0).astype(np.float32)
    pos = positions.reshape(t, 1)
    kcol = A_Q // A_KV
    prev = lambda i, s: (jnp.maximum(i - 1, 0), 0)
    return pl.pallas_call(
        functools.partial(_attn_kernel, blocks_per_seq=seq_len // BLOCK),
        out_shape=jax.ShapeDtypeStruct((t, A_Q), BF16),
        grid_spec=pltpu.PrefetchScalarGridSpec(
            num_scalar_prefetch=1,
            grid=(nb,),
            in_specs=[
                pl.BlockSpec((BLOCK, 1), lambda i, s: (i, 0)),
                pl.BlockSpec((BLOCK, 1), prev),
                pl.BlockSpec((1, V7X_LANES), lambda i, s: (0, 0)),
                pl.BlockSpec((BLOCK, A_Q), lambda i, s: (i, 0)),
                pl.BlockSpec((BLOCK, A_KV), lambda i, s: (i, kcol)),
                pl.BlockSpec((BLOCK, A_KV), lambda i, s: (jnp.maximum(i - 1, 0), kcol)),
                pl.BlockSpec((BLOCK, A_KV), lambda i, s: (i, kcol + 1)),
                pl.BlockSpec((BLOCK, A_KV), lambda i, s: (jnp.maximum(i - 1, 0), kcol + 1)),
            ],
            out_specs=pl.BlockSpec((BLOCK, A_Q), lambda i, s: (i, 0)),
        ),
        compiler_params=_params("parallel"),
        name="swa_attention",
    )(sinks.astype(F32), pos, pos, jnp.asarray(invf_row).reshape(1, V7X_LANES),
      proj, proj, proj, proj, proj)


def _conv_kernel(a0_ref, a1_ref, g0_ref, g1_ref, w_ref, cb_ref, lg_ref, lb_ref, o_ref,
                 hbuf, ybuf, hsh, *, steps_per_seq):
    i = pl.program_id(0)
    ts = o_ref.shape[0]
    half = CONV_CH // 2

    @pl.when(i % steps_per_seq == 0)
    def _():
        hbuf[0:CONV_HALO, :] = jnp.zeros((CONV_HALO, CONV_CH), F32)

    @pl.when(i % steps_per_seq != 0)
    def _():
        hbuf[0:CONV_HALO, :] = hbuf[ts:ts + CONV_HALO, :]

    for c, (a_ref, g_ref) in enumerate(((a0_ref, g0_ref), (a1_ref, g1_ref))):
        hbuf[CONV_HALO:CONV_HALO + ts, c * half:(c + 1) * half] = (
            a_ref[...].astype(F32) * _sigmoid(g_ref[...].astype(F32)))

    rows = V7X_SUBLANES
    first_tap = CONV_HALO - (CONV_WIDTH - 1)
    h_all = hbuf[...]
    hsh[0] = h_all
    for s in range(1, rows):
        hsh[s] = pltpu.roll(h_all, ts + CONV_HALO - s, 0)

    def body(r, carry):
        base = pl.multiple_of(r * rows, rows)
        acc = jnp.broadcast_to(cb_ref[...], (rows, CONV_CH))
        for j in range(CONV_WIDTH):
            off = first_tap + j
            tap = hsh[off % rows, pl.ds(base + off // rows * rows, rows), :]
            acc = acc + tap * w_ref[j:j + 1, :]
        ybuf[pl.ds(base, rows), :] = acc
        return carry

    lax.fori_loop(0, ts // rows, body, 0)

    y = ybuf[...]
    mu = jnp.mean(y, axis=-1, keepdims=True)
    yc = y - mu
    var = jnp.mean(yc * yc, axis=-1, keepdims=True)
    z = yc * lax.rsqrt(var + EPS) * lg_ref[...] + lb_ref[...]
    o_ref[...] = (z * _sigmoid(z)).astype(o_ref.dtype)


def conformer_conv(proj, conv_w, conv_b, ln_g, ln_b, seq_len):
    t = proj.shape[0]
    ts = _tile(seq_len, 256)
    half = CONV_CH // 2
    col0 = (A_Q + 2 * A_KV) // half
    row = lambda v: v.reshape(1, CONV_CH).astype(F32)
    full = lambda shape: pl.BlockSpec(shape, lambda i: (0, 0))
    return pl.pallas_call(
        functools.partial(_conv_kernel, steps_per_seq=seq_len // ts),
        out_shape=jax.ShapeDtypeStruct((t, CONV_CH), BF16),
        grid=(t // ts,),
        in_specs=[
            pl.BlockSpec((ts, half), lambda i: (i, col0)),
            pl.BlockSpec((ts, half), lambda i: (i, col0 + 1)),
            pl.BlockSpec((ts, half), lambda i: (i, col0 + 2)),
            pl.BlockSpec((ts, half), lambda i: (i, col0 + 3)),
            full((CONV_WIDTH, CONV_CH)),
            full((1, CONV_CH)), full((1, CONV_CH)), full((1, CONV_CH)),
        ],
        out_specs=pl.BlockSpec((ts, CONV_CH), lambda i: (i, 0)),
        scratch_shapes=[pltpu.VMEM((ts + CONV_HALO, CONV_CH), F32), pltpu.VMEM((ts, CONV_CH), F32),
                        pltpu.VMEM((V7X_SUBLANES, ts + CONV_HALO, CONV_CH), F32)],
        compiler_params=_params("arbitrary"),
        name="conformer_conv",
    )(proj, proj, proj, proj, conv_w.astype(F32), row(conv_b), row(ln_g), row(ln_b))


def _matmul_res_kernel(*refs, n_a):
    a_refs, w_refs = refs[:n_a], refs[n_a:2 * n_a]
    r_ref, o_ref = refs[2 * n_a], refs[2 * n_a + 1]
    acc = r_ref[...]
    for a_ref, w_ref in zip(a_refs, w_refs):
        acc = acc + jnp.dot(a_ref[...], w_ref[...], preferred_element_type=F32)
    o_ref[...] = acc


def matmul_res(a_list, w_list, res, *, name):
    t, n = res.shape
    tm = _tile(t, 512)
    tn = _tile(n, 1024)
    n_a = len(a_list)
    in_specs = [pl.BlockSpec((tm, a.shape[1]), lambda i, j: (i, 0)) for a in a_list]
    in_specs += [pl.BlockSpec((w.shape[0], tn), lambda i, j: (0, j)) for w in w_list]
    in_specs += [pl.BlockSpec((tm, tn), lambda i, j: (i, j))]
    return pl.pallas_call(
        functools.partial(_matmul_res_kernel, n_a=n_a),
        out_shape=jax.ShapeDtypeStruct((t, n), F32),
        grid=(t // tm, n // tn),
        in_specs=in_specs,
        out_specs=pl.BlockSpec((tm, tn), lambda i, j: (i, j)),
        compiler_params=_params("parallel", "arbitrary"),
        name=name,
    )(*a_list, *w_list, res)


def _sgu_kernel(u_ref, v_ref, lg_ref, lb_ref, ws_ref, bs_ref, o_ref):
    rows = o_ref.shape[0]
    v = v_ref[...].astype(F32)
    mu = jnp.mean(v, axis=-1, keepdims=True)
    vc = v - mu
    var = jnp.mean(vc * vc, axis=-1, keepdims=True)
    vn = (vc * lax.rsqrt(var + EPS) * lg_ref[...] + lb_ref[...]).astype(BF16)
    ti = lax.broadcasted_iota(jnp.int32, (CHUNK, CHUNK), 0)
    si = lax.broadcasted_iota(jnp.int32, (CHUNK, CHUNK), 1)
    causal = si <= ti
    for c in range(rows // CHUNK):
        rsl = slice(c * CHUNK, (c + 1) * CHUNK)
        for g in range(GMLP_GROUPS):
            csl = slice(g * GMLP_GROUP_DIM, (g + 1) * GMLP_GROUP_DIM)
            ws = jnp.where(causal, ws_ref[g], 0.0).astype(BF16)
            sv = jnp.dot(ws, vn[rsl, csl], preferred_element_type=F32) + bs_ref[:, g:g + 1]
            o_ref[rsl, csl] = (u_ref[rsl, csl].astype(F32) * sv).astype(o_ref.dtype)


def spatial_gating(z, ln_g, ln_b, w_s, b_s):
    t = z.shape[0]
    rows = CHUNK
    row = lambda v: v.reshape(1, GMLP_DIM).astype(F32)
    return pl.pallas_call(
        _sgu_kernel,
        out_shape=jax.ShapeDtypeStruct((t, GMLP_DIM), BF16),
        grid=(t // rows,),
        in_specs=[
            pl.BlockSpec((rows, GMLP_DIM), lambda i: (i, 0)),
            pl.BlockSpec((rows, GMLP_DIM), lambda i: (i, 1)),
            pl.BlockSpec((1, GMLP_DIM), lambda i: (0, 0)),
            pl.BlockSpec((1, GMLP_DIM), lambda i: (0, 0)),
            pl.BlockSpec((GMLP_GROUPS, CHUNK, CHUNK), lambda i: (0, 0, 0)),
            pl.BlockSpec((CHUNK, GMLP_GROUPS), lambda i: (0, 0)),
        ],
        out_specs=pl.BlockSpec((rows, GMLP_DIM), lambda i: (i, 0)),
        compiler_params=_params("parallel"),
        name="spatial_gating",
    )(z, z, row(ln_g), row(ln_b), w_s.astype(F32), b_s.T.astype(F32))


ROUTER_LANES = V7X_LANES


def _first_argmax(x, lane, width):
    m = jnp.max(x, axis=-1, keepdims=True)
    idx = jnp.min(jnp.where(x == m, lane, width), axis=-1, keepdims=True)
    return m, idx


def _router_kernel(x_ref, g_ref, w_ref, b_ref, hn_ref, eid_ref, gate_ref, rank_ref, cnt_ref, carry):
    i = pl.program_id(0)
    tm = x_ref.shape[0]

    @pl.when(i == 0)
    def _():
        carry[...] = jnp.zeros_like(carry)

    h = _rmsnorm_rows(x_ref[...], g_ref[...])
    hn_ref[...] = h
    logits = jnp.dot(h, w_ref[...], preferred_element_type=F32,
                     precision=lax.Precision.HIGHEST) + b_ref[...]
    lane = lax.broadcasted_iota(jnp.int32, logits.shape, 1)
    gl = jnp.where(lane < N_GROUPS, logits, NEG_INF)
    gmax, g_idx = _first_argmax(gl, lane, ROUTER_LANES)
    g_gate = 1.0 / jnp.sum(jnp.exp(gl - gmax), axis=-1, keepdims=True)
    e_lane = lane - N_GROUPS
    in_group = (e_lane >= g_idx * EXPERTS_PER_GROUP) & (e_lane < (g_idx + 1) * EXPERTS_PER_GROUP)
    el = jnp.where(in_group, logits, NEG_INF)
    l1, i1 = _first_argmax(el, lane, ROUTER_LANES)
    el2 = jnp.where(lane == i1, NEG_INF, el)
    l2, i2 = _first_argmax(el2, lane, ROUTER_LANES)
    r = jnp.exp(l2 - l1)
    w1 = g_gate / (1.0 + r)
    w2 = g_gate * r / (1.0 + r)
    e1 = i1 - N_GROUPS
    e2 = i2 - N_GROUPS
    hot1 = lane == e1
    hot2 = lane == e2
    hot = (hot1 | hot2).astype(BF16)
    ri = lax.broadcasted_iota(jnp.int32, (tm, tm), 0)
    ci = lax.broadcasted_iota(jnp.int32, (tm, tm), 1)
    before = (ci < ri).astype(BF16)
    seen = jnp.dot(before, hot, preferred_element_type=F32) + carry[...]
    r1 = jnp.sum(jnp.where(hot1, seen, 0.0), axis=-1, keepdims=True)
    r2 = jnp.sum(jnp.where(hot2, seen, 0.0), axis=-1, keepdims=True)
    carry[...] = carry[...] + jnp.sum(hot.astype(F32), axis=0, keepdims=True)
    eid_ref[...] = jnp.concatenate([e1, e2], axis=1)
    gate_ref[...] = jnp.concatenate([w1, w2], axis=1)
    rank_ref[...] = jnp.concatenate([r1, r2], axis=1).astype(jnp.int32)
    cnt_ref[...] = carry[...].astype(jnp.int32)


def moe_router(x, norm, w_rg, b_rg, w_re, b_re):
    t, d = x.shape
    tm = _tile(t, 256)
    pad = ROUTER_LANES - N_GROUPS - N_EXPERTS
    w = jnp.concatenate([w_rg, w_re, jnp.zeros((d, pad), F32)], axis=1)
    b = jnp.concatenate([b_rg, b_re, jnp.zeros((pad,), F32)]).reshape(1, ROUTER_LANES)
    pair = lambda dt: jax.ShapeDtypeStruct((t, 2), dt)
    pair_spec = pl.BlockSpec((tm, 2), lambda i: (i, 0))
    return pl.pallas_call(
        _router_kernel,
        out_shape=(jax.ShapeDtypeStruct((t, d), F32), pair(jnp.int32), pair(F32), pair(jnp.int32),
                   jax.ShapeDtypeStruct((1, ROUTER_LANES), jnp.int32)),
        grid=(t // tm,),
        in_specs=[
            pl.BlockSpec((tm, d), lambda i: (i, 0)),
            pl.BlockSpec((1, d), lambda i: (0, 0)),
            pl.BlockSpec((d, ROUTER_LANES), lambda i: (0, 0)),
            pl.BlockSpec((1, ROUTER_LANES), lambda i: (0, 0)),
        ],
        out_specs=(pl.BlockSpec((tm, d), lambda i: (i, 0)), pair_spec, pair_spec, pair_spec,
                   pl.BlockSpec((1, ROUTER_LANES), lambda i: (0, 0))),
        scratch_shapes=[pltpu.VMEM((1, ROUTER_LANES), F32)],
        compiler_params=_params("arbitrary"),
        name="moe_router",
    )(x, norm.reshape(1, d), w, b)


def _row_copy(src_hbm, dst, src_row, dst_row, sem):
    return pltpu.make_async_copy(src_hbm.at[pl.ds(src_row, 1)], dst.at[pl.ds(dst_row, 1)], sem)


def _experts_kernel(te_ref, na_ref, tok_ref, hn_hbm, wg_ref, wu_ref, wd_ref, ys_ref, hbuf, sems):
    i = pl.program_id(0)
    tm = ys_ref.shape[0]
    n_active = na_ref[0]

    def gather(tile, slot):
        def body(r, carry):
            _row_copy(hn_hbm, hbuf.at[slot], tok_ref[tile * tm + r], r, sems.at[slot]).start()
            return carry
        lax.fori_loop(0, tm, body, 0, unroll=8)

    @pl.when(i == 0)
    def _():
        gather(0, 0)

    @pl.when(i + 1 < n_active)
    def _():
        gather(i + 1, (i + 1) % 2)

    @pl.when(i < n_active)
    def _():
        slot = i % 2
        pltpu.make_async_copy(hn_hbm.at[pl.ds(0, tm)], hbuf.at[slot], sems.at[slot]).wait()
        h = hbuf[slot].astype(BF16)
        g = jnp.dot(h, wg_ref[...], preferred_element_type=F32)
        u = jnp.dot(h, wu_ref[...], preferred_element_type=F32)
        act = (g * _sigmoid(g) * u).astype(BF16)
        ys_ref[...] = jnp.dot(act, wd_ref[...], preferred_element_type=F32)

    @pl.when(i >= n_active)
    def _():
        ys_ref[...] = jnp.zeros_like(ys_ref)


def moe_experts(hn, tok_of_slot, tile_expert, n_active, w_gate, w_up, w_down, tm):
    d = hn.shape[1]
    n_slots = tok_of_slot.shape[0]
    w_map = lambda i, te, na, tok: (te[i], 0, 0)
    return pl.pallas_call(
        _experts_kernel,
        out_shape=jax.ShapeDtypeStruct((n_slots, d), F32),
        grid_spec=pltpu.PrefetchScalarGridSpec(
            num_scalar_prefetch=3,
            grid=(n_slots // tm,),
            in_specs=[
                pl.BlockSpec(memory_space=pl.ANY),
                pl.BlockSpec((None, d, D_EXPERT), w_map),
                pl.BlockSpec((None, d, D_EXPERT), w_map),
                pl.BlockSpec((None, D_EXPERT, d), w_map),
            ],
            out_specs=pl.BlockSpec((tm, d), lambda i, te, na, tok: (i, 0)),
            scratch_shapes=[pltpu.VMEM((2, tm, d), F32), pltpu.SemaphoreType.DMA((2,))],
        ),
        compiler_params=_params("arbitrary"),
        name="moe_experts",
    )(tile_expert, n_active, tok_of_slot, hn, w_gate, w_up, w_down)


def _combine_kernel(dest_ref, x_ref, gate_ref, ys_hbm, *rest, final_norm):
    if final_norm:
        fg_ref, o_ref, ybuf, sem = rest
    else:
        o_ref, ybuf, sem = rest
    tc = x_ref.shape[0]
    base = pl.program_id(0) * tc

    def start(k, carry):
        tok = base + k
        _row_copy(ys_hbm, ybuf.at[0], dest_ref[2 * tok], k, sem).start()
        _row_copy(ys_hbm, ybuf.at[1], dest_ref[2 * tok + 1], k, sem).start()
        return carry

    def wait(k, carry):
        _row_copy(ys_hbm, ybuf.at[0], 0, 0, sem).wait()
        _row_copy(ys_hbm, ybuf.at[1], 0, 0, sem).wait()
        return carry

    lax.fori_loop(0, tc, start, 0)
    lax.fori_loop(0, tc, wait, 0)
    gate = gate_ref[...]
    y = x_ref[...] + gate[:, 0:1] * ybuf[0] + gate[:, 1:2] * ybuf[1]
    if final_norm:
        y = _rmsnorm_rows(y, fg_ref[...])
    o_ref[...] = y


def moe_combine(x, gates, dest, ys, final_g=None):
    t, d = x.shape
    tc = _tile(t, 256)
    final_norm = final_g is not None
    in_specs = [
        pl.BlockSpec((tc, d), lambda i, dst: (i, 0)),
        pl.BlockSpec((tc, 2), lambda i, dst: (i, 0)),
        pl.BlockSpec(memory_space=pl.ANY),
    ]
    args = [x, gates, ys]
    if final_norm:
        in_specs.append(pl.BlockSpec((1, d), lambda i, dst: (0, 0)))
        args.append(final_g.reshape(1, d))
    return pl.pallas_call(
        functools.partial(_combine_kernel, final_norm=final_norm),
        out_shape=jax.ShapeDtypeStruct((t, d), F32),
        grid_spec=pltpu.PrefetchScalarGridSpec(
            num_scalar_prefetch=1,
            grid=(t // tc,),
            in_specs=in_specs,
            out_specs=pl.BlockSpec((tc, d), lambda i, dst: (i, 0)),
            scratch_shapes=[pltpu.VMEM((2, tc, d), F32), pltpu.SemaphoreType.DMA],
        ),
        compiler_params=_params("arbitrary"),
        name="moe_combine",
    )(dest, *args)


def hierarchical_moe(x, norm, w_rg, b_rg, w_re, b_re, w_gate, w_up, w_down, final_g=None):
    t, d = x.shape
    tm = _tile(t, 256)
    n_slots = 2 * t + N_EXPERTS * tm
    hn, eid, gates, rank, counts = moe_router(x, norm, w_rg, b_rg, w_re, b_re)
    counts = counts[0, :N_EXPERTS]
    padded = (counts + tm - 1) // tm * tm
    ends = jnp.cumsum(padded)
    dest = ((ends - padded)[eid] + rank).reshape(2 * t).astype(jnp.int32)
    n_active = (ends[-1] // tm).astype(jnp.int32)
    tile_row = jnp.minimum(jnp.arange(n_slots // tm, dtype=jnp.int32), n_active - 1) * tm
    tile_expert = jnp.sum(tile_row[:, None] >= ends[None, :], axis=1).astype(jnp.int32)
    tok_of_slot = jnp.zeros((n_slots,), jnp.int32).at[dest].set(
        jnp.arange(2 * t, dtype=jnp.int32) // 2, unique_indices=True)
    ys = moe_experts(hn, tok_of_slot, tile_expert, n_active.reshape(1), w_gate, w_up, w_down, tm)
    return moe_combine(x, gates, dest, ys, final_g)


def kernel(x, positions, l0_norm_mix, l0_w_in, l0_sinks, l0_conv_w, l0_conv_b, l0_ln_g, l0_ln_b, l0_w_out, l0_norm_ffn, l0_w_rg, l0_b_rg, l0_w_re, l0_b_re, l0_w_gate, l0_w_up, l0_w_down, l1_norm_mix, l1_w_in, l1_ln_g, l1_ln_b, l1_w_s, l1_b_s, l1_w_out, l1_norm_ffn, l1_w_rg, l1_b_rg, l1_w_re, l1_b_re, l1_w_gate, l1_w_up, l1_w_down, final_norm):
    b, s, d = x.shape
    t = b * s
    x = x.reshape(t, d)
    bf = lambda w: w.astype(BF16)

    proj = rms_matmul(x, l0_norm_mix, bf(l0_w_in), name="l0_in_proj")
    attn = attention(proj, positions, l0_sinks, s)
    conv = conformer_conv(proj, l0_conv_w, l0_conv_b, l0_ln_g, l0_ln_b, s)
    w_out0 = bf(l0_w_out)
    x = matmul_res([attn, conv], [w_out0[:A_Q], w_out0[A_Q:]], x, name="l0_out_proj")
    x = hierarchical_moe(x, l0_norm_ffn, l0_w_rg, l0_b_rg, l0_w_re, l0_b_re,
                         bf(l0_w_gate), bf(l0_w_up), bf(l0_w_down))
    z = rms_matmul(x, l1_norm_mix, bf(l1_w_in), act="gelu", name="l1_in_proj")
    gated = spatial_gating(z, l1_ln_g, l1_ln_b, l1_w_s, l1_b_s)
    x = matmul_res([gated], [bf(l1_w_out)], x, name="l1_out_proj")
    x = hierarchical_moe(x, l1_norm_ffn, l1_w_rg, l1_b_rg, l1_w_re, l1_b_re,
                         bf(l1_w_gate), bf(l1_w_up), bf(l1_w_down), final_g=final_norm)
    return x.reshape(b, s, d)
```

```python
import functools

import numpy as np
import jax
import jax.numpy as jnp
from jax import lax
from jax.experimental import pallas as pl
from jax.experimental.pallas import tpu as pltpu

D_MODEL = 4096
HEAD_DIM = 64
N_Q_HEADS = 32
N_KV_HEADS = 8
Q_PER_KV = N_Q_HEADS // N_KV_HEADS
BLOCK = 128
ROPE_THETA = 500000.0
ROPE_DIM = HEAD_DIM // 4
ROPE_HALF = ROPE_DIM // 2
A_Q = N_Q_HEADS * HEAD_DIM
A_KV = N_KV_HEADS * HEAD_DIM
CONV_CH = D_MODEL // 2
CONV_WIDTH = 31
CONV_HALO = 32
IN_EVEN = A_Q + 2 * A_KV + 2 * CONV_CH
CHUNK = 128
GMLP_DIM = D_MODEL
GMLP_GROUPS = 16
GMLP_GROUP_DIM = GMLP_DIM // GMLP_GROUPS
N_GROUPS = 4
EXPERTS_PER_GROUP = 8
N_EXPERTS = N_GROUPS * EXPERTS_PER_GROUP
D_EXPERT = 512
EPS = 1e-5
NEG_INF = -1e30

V7X_LANES = 128
V7X_SUBLANES = 8
V7X_VMEM_BYTES = 64 * 1024 * 1024
VMEM_LIMIT = 56 * 1024 * 1024

F32 = jnp.float32
BF16 = jnp.bfloat16


def _tile(n, want):
    t = min(n, want)
    while n % t:
        t //= 2
    return t


def _params(*sem):
    return pltpu.CompilerParams(dimension_semantics=sem, vmem_limit_bytes=VMEM_LIMIT)


def _sigmoid(x):
    return 1.0 / (1.0 + jnp.exp(-x))


def _gelu_tanh(x):
    c = np.float32(np.sqrt(2.0 / np.pi))
    return x * (0.5 * (1.0 + jnp.tanh(c * (x + 0.044715 * (x * x * x)))))


def _rmsnorm_rows(x, g):
    ms = jnp.mean(x * x, axis=-1, keepdims=True)
    return x * lax.rsqrt(ms + EPS) * g


def _rms_matmul_kernel(x_ref, g_ref, w_ref, o_ref, hn_ref, *, act):
    @pl.when(pl.program_id(1) == 0)
    def _():
        hn_ref[...] = _rmsnorm_rows(x_ref[...], g_ref[...]).astype(hn_ref.dtype)

    acc = jnp.dot(hn_ref[...], w_ref[...], preferred_element_type=F32)
    if act == "gelu":
        acc = _gelu_tanh(acc)
    o_ref[...] = acc.astype(o_ref.dtype)


def rms_matmul(x, g, w, *, act=None, out_dtype=BF16, name):
    t, d = x.shape
    n = w.shape[1]
    tm = _tile(t, 512)
    tn = _tile(n, 1024)
    return pl.pallas_call(
        functools.partial(_rms_matmul_kernel, act=act),
        out_shape=jax.ShapeDtypeStruct((t, n), out_dtype),
        grid=(t // tm, n // tn),
        in_specs=[
            pl.BlockSpec((tm, d), lambda i, j: (i, 0)),
            pl.BlockSpec((1, d), lambda i, j: (0, 0)),
            pl.BlockSpec((d, tn), lambda i, j: (0, j)),
        ],
        out_specs=pl.BlockSpec((tm, tn), lambda i, j: (i, j)),
        scratch_shapes=[pltpu.VMEM((tm, d), BF16)],
        compiler_params=_params("parallel", "arbitrary"),
        name=name,
    )(x, g.reshape(1, d), w)


def _rope_tables(pos_ref, invf_ref):
    ang = pos_ref[...].astype(F32) * invf_ref[...]
    c = jnp.cos(ang)
    s = jnp.sin(ang)
    lane = lax.broadcasted_iota(jnp.int32, ang.shape, 1) % HEAD_DIM
    s_lo = jnp.where(lane < ROPE_HALF, -s, 0.0)
    s_hi = jnp.where((lane >= ROPE_HALF) & (lane < ROPE_DIM), s, 0.0)
    return c, s_lo, s_hi


def _apply_rope(x, tables):
    c, s_lo, s_hi = tables
    w = x.shape[1]
    reps = w // c.shape[1]
    c, s_lo, s_hi = (jnp.concatenate([t] * reps, axis=1) for t in (c, s_lo, s_hi))
    return (x * c + pltpu.roll(x, w - ROPE_HALF, 1) * s_lo + pltpu.roll(x, ROPE_HALF, 1) * s_hi)


def _attn_kernel(sink_ref, pos_c_ref, pos_p_ref, invf_ref, q_ref, kc_ref, kp_ref, vc_ref, vp_ref,
                 o_ref, *, blocks_per_seq):
    i = pl.program_id(0)
    not_first = (i % blocks_per_seq) > 0
    tab_c = _rope_tables(pos_c_ref, invf_ref)
    tab_p = _rope_tables(pos_p_ref, invf_ref)
    scale = HEAD_DIM ** -0.5
    q = _apply_rope(q_ref[...].astype(F32), tab_c) * scale
    k = jnp.concatenate([_apply_rope(kp_ref[...].astype(F32), tab_p),
                         _apply_rope(kc_ref[...].astype(F32), tab_c)], axis=0)
    v = jnp.concatenate([vp_ref[...], vc_ref[...]], axis=0).astype(F32)

    qi = lax.broadcasted_iota(jnp.int32, (Q_PER_KV * BLOCK, 2 * BLOCK), 0) % BLOCK + BLOCK
    kj = lax.broadcasted_iota(jnp.int32, (Q_PER_KV * BLOCK, 2 * BLOCK), 1)
    valid = (kj <= qi) & (qi - kj < BLOCK) & (not_first | (kj >= BLOCK))

    col = V7X_LANES
    low_q = lax.broadcasted_iota(jnp.int32, (BLOCK, col), 1) < HEAD_DIM
    low_kv = lax.broadcasted_iota(jnp.int32, (2 * BLOCK, col), 1) < HEAD_DIM
    ones = jnp.ones((2 * BLOCK, col), BF16)
    outs = []
    for c in range(A_KV // col):
        kcol = k[:, c * col:(c + 1) * col]
        vcol = v[:, c * col:(c + 1) * col]
        kswap = pltpu.roll(kcol, HEAD_DIM, 1)
        vswap = pltpu.roll(vcol, HEAD_DIM, 1)
        for half in range(col // HEAD_DIM):
            g = c * (col // HEAD_DIM) + half
            if half == 0:
                kk, vv = jnp.where(low_kv, kcol, kswap), jnp.where(low_kv, vcol, vswap)
            else:
                kk, vv = jnp.where(low_kv, kswap, kcol), jnp.where(low_kv, vswap, vcol)
            vv_ones = jnp.concatenate([vv.astype(BF16), ones], axis=1)
            qa = q[:, (2 * g) * col:(2 * g + 1) * col]
            qb = q[:, (2 * g + 1) * col:(2 * g + 2) * col]
            qs = jnp.concatenate([jnp.where(low_q, qa, 0.0), jnp.where(low_q, 0.0, qa),
                                  jnp.where(low_q, qb, 0.0), jnp.where(low_q, 0.0, qb)],
                                 axis=0).astype(BF16)
            s = lax.dot_general(qs, kk.astype(BF16), (((1,), (1,)), ((), ())),
                                preferred_element_type=F32)
            s = jnp.where(valid, s, NEG_INF)
            sink = jnp.concatenate(
                [jnp.full((BLOCK, 1), sink_ref[g * Q_PER_KV + h], F32) for h in range(Q_PER_KV)], axis=0)
            m = jnp.maximum(jnp.max(s, axis=-1, keepdims=True), sink)
            p = jnp.exp(s - m).astype(BF16)
            o = jnp.dot(p, vv_ones, preferred_element_type=F32)
            o = o[:, :col] / (o[:, col:] + jnp.exp(sink - m))
            outs.append(jnp.where(low_q, o[0:BLOCK], o[BLOCK:2 * BLOCK]))
            outs.append(jnp.where(low_q, o[2 * BLOCK:3 * BLOCK], o[3 * BLOCK:4 * BLOCK]))
    o_ref[...] = jnp.concatenate(outs, axis=1).astype(o_ref.dtype)


def attention(proj, positions, sinks, seq_len):
    t = proj.shape[0]
    nb = t // BLOCK
    invf = ROPE_THETA ** (-np.arange(0, ROPE_DIM, 2, dtype=np.float32) / ROPE_DIM)
    lane = np.arange(V7X_LANES) % HEAD_DIM
    invf_row = np.where(lane < ROPE_DIM, invf[lane % ROPE_HALF], 0.0).astype(np.float32)
    pos = positions.reshape(t, 1)
    kcol = A_Q // A_KV
    prev = lambda i, s: (jnp.maximum(i - 1, 0), 0)
    return pl.pallas_call(
        functools.partial(_attn_kernel, blocks_per_seq=seq_len // BLOCK),
        out_shape=jax.ShapeDtypeStruct((t, A_Q), BF16),
        grid_spec=pltpu.PrefetchScalarGridSpec(
            num_scalar_prefetch=1,
            grid=(nb,),
            in_specs=[
                pl.BlockSpec((BLOCK, 1), lambda i, s: (i, 0)),
                pl.BlockSpec((BLOCK, 1), prev),
                pl.BlockSpec((1, V7X_LANES), lambda i, s: (0, 0)),
                pl.BlockSpec((BLOCK, A_Q), lambda i, s: (i, 0)),
                pl.BlockSpec((BLOCK, A_KV), lambda i, s: (i, kcol)),
                pl.BlockSpec((BLOCK, A_KV), lambda i, s: (jnp.maximum(i - 1, 0), kcol)),
                pl.BlockSpec((BLOCK, A_KV), lambda i, s: (i, kcol + 1)),
                pl.BlockSpec((BLOCK, A_KV), lambda i, s: (jnp.maximum(i - 1, 0), kcol + 1)),
            ],
            out_specs=pl.BlockSpec((BLOCK, A_Q), lambda i, s: (i, 0)),
        ),
        compiler_params=_params("parallel"),
        name="swa_attention",
    )(sinks.astype(F32), pos, pos, jnp.asarray(invf_row).reshape(1, V7X_LANES),
      proj, proj, proj, proj, proj)


def _conv_kernel(a0_ref, a1_ref, g0_ref, g1_ref, w_ref, cb_ref, lg_ref, lb_ref, o_ref,
                 hbuf, ybuf, hsh, *, steps_per_seq):
    i = pl.program_id(0)
    ts = o_ref.shape[0]
    half = CONV_CH // 2

    @pl.when(i % steps_per_seq == 0)
    def _():
        hbuf[0:CONV_HALO, :] = jnp.zeros((CONV_HALO, CONV_CH), F32)

    @pl.when(i % steps_per_seq != 0)
    def _():
        hbuf[0:CONV_HALO, :] = hbuf[ts:ts + CONV_HALO, :]

    for c, (a_ref, g_ref) in enumerate(((a0_ref, g0_ref), (a1_ref, g1_ref))):
        hbuf[CONV_HALO:CONV_HALO + ts, c * half:(c + 1) * half] = (
            a_ref[...].astype(F32) * _sigmoid(g_ref[...].astype(F32)))

    sub = V7X_SUBLANES
    first_tap = CONV_HALO - (CONV_WIDTH - 1)
    h_all = hbuf[...]
    for s in range(1, sub):
        hsh[s - 1] = pltpu.roll(h_all, ts + CONV_HALO - s, 0)

    rows = 2 * sub
    lanes = CONV_CH // 4

    def body(r, carry):
        base = pl.multiple_of(r * rows, rows)
        for c in range(CONV_CH // lanes):
            csl = slice(c * lanes, (c + 1) * lanes)
            acc = [jnp.broadcast_to(cb_ref[:, csl], (sub, lanes)) for _ in range(rows // sub)]
            for j in range(CONV_WIDTH):
                off = first_tap + j
                w = w_ref[j * sub:(j + 1) * sub, csl]
                for b in range(rows // sub):
                    start = base + (off // sub + b) * sub
                    if off % sub == 0:
                        tap = hbuf[pl.ds(start, sub), csl]
                    else:
                        tap = hsh[off % sub - 1, pl.ds(start, sub), csl]
                    acc[b] = acc[b] + tap * w
            for b in range(rows // sub):
                ybuf[pl.ds(base + b * sub, sub), csl] = acc[b]
        return carry

    lax.fori_loop(0, ts // rows, body, 0)

    y = ybuf[...]
    mu = jnp.mean(y, axis=-1, keepdims=True)
    yc = y - mu
    var = jnp.mean(yc * yc, axis=-1, keepdims=True)
    z = yc * lax.rsqrt(var + EPS) * lg_ref[...] + lb_ref[...]
    o_ref[...] = (z * _sigmoid(z)).astype(o_ref.dtype)


def conformer_conv(proj, conv_w, conv_b, ln_g, ln_b, seq_len):
    t = proj.shape[0]
    ts = _tile(seq_len, 256)
    half = CONV_CH // 2
    col0 = (A_Q + 2 * A_KV) // half
    row = lambda v: v.reshape(1, CONV_CH).astype(F32)
    full = lambda shape: pl.BlockSpec(shape, lambda i: (0, 0))
    return pl.pallas_call(
        functools.partial(_conv_kernel, steps_per_seq=seq_len // ts),
        out_shape=jax.ShapeDtypeStruct((t, CONV_CH), BF16),
        grid=(t // ts,),
        in_specs=[
            pl.BlockSpec((ts, half), lambda i: (i, col0)),
            pl.BlockSpec((ts, half), lambda i: (i, col0 + 1)),
            pl.BlockSpec((ts, half), lambda i: (i, col0 + 2)),
            pl.BlockSpec((ts, half), lambda i: (i, col0 + 3)),
            full((CONV_WIDTH * V7X_SUBLANES, CONV_CH)),
            full((1, CONV_CH)), full((1, CONV_CH)), full((1, CONV_CH)),
        ],
        out_specs=pl.BlockSpec((ts, CONV_CH), lambda i: (i, 0)),
        scratch_shapes=[pltpu.VMEM((ts + CONV_HALO, CONV_CH), F32), pltpu.VMEM((ts, CONV_CH), F32),
                        pltpu.VMEM((V7X_SUBLANES - 1, ts + CONV_HALO, CONV_CH), F32)],
        compiler_params=_params("arbitrary"),
        name="conformer_conv",
    )(proj, proj, proj, proj, jnp.repeat(conv_w.astype(F32), V7X_SUBLANES, axis=0),
      row(conv_b), row(ln_g), row(ln_b))


def _matmul_res_kernel(*refs, n_a):
    a_refs, w_refs = refs[:n_a], refs[n_a:2 * n_a]
    r_ref, o_ref = refs[2 * n_a], refs[2 * n_a + 1]
    acc = r_ref[...]
    for a_ref, w_ref in zip(a_refs, w_refs):
        acc = acc + jnp.dot(a_ref[...], w_ref[...], preferred_element_type=F32)
    o_ref[...] = acc


def matmul_res(a_list, w_list, res, *, name):
    t, n = res.shape
    tm = _tile(t, 512)
    tn = _tile(n, 1024)
    n_a = len(a_list)
    in_specs = [pl.BlockSpec((tm, a.shape[1]), lambda i, j: (i, 0)) for a in a_list]
    in_specs += [pl.BlockSpec((w.shape[0], tn), lambda i, j: (0, j)) for w in w_list]
    in_specs += [pl.BlockSpec((tm, tn), lambda i, j: (i, j))]
    return pl.pallas_call(
        functools.partial(_matmul_res_kernel, n_a=n_a),
        out_shape=jax.ShapeDtypeStruct((t, n), F32),
        grid=(t // tm, n // tn),
        in_specs=in_specs,
        out_specs=pl.BlockSpec((tm, tn), lambda i, j: (i, j)),
        compiler_params=_params("parallel", "arbitrary"),
        name=name,
    )(*a_list, *w_list, res)


def _sgu_kernel(u_ref, v_ref, lg_ref, lb_ref, ws_ref, bs_ref, o_ref):
    rows = o_ref.shape[0]
    v = v_ref[...].astype(F32)
    mu = jnp.mean(v, axis=-1, keepdims=True)
    vc = v - mu
    var = jnp.mean(vc * vc, axis=-1, keepdims=True)
    vn = (vc * lax.rsqrt(var + EPS) * lg_ref[...] + lb_ref[...]).astype(BF16)
    ti = lax.broadcasted_iota(jnp.int32, (CHUNK, CHUNK), 0)
    si = lax.broadcasted_iota(jnp.int32, (CHUNK, CHUNK), 1)
    causal = si <= ti
    for c in range(rows // CHUNK):
        rsl = slice(c * CHUNK, (c + 1) * CHUNK)
        for g in range(GMLP_GROUPS):
            csl = slice(g * GMLP_GROUP_DIM, (g + 1) * GMLP_GROUP_DIM)
            ws = jnp.where(causal, ws_ref[g], 0.0).astype(BF16)
            sv = jnp.dot(ws, vn[rsl, csl], preferred_element_type=F32) + bs_ref[:, g:g + 1]
            o_ref[rsl, csl] = (u_ref[rsl, csl].astype(F32) * sv).astype(o_ref.dtype)


def spatial_gating(z, ln_g, ln_b, w_s, b_s):
    t = z.shape[0]
    rows = CHUNK
    row = lambda v: v.reshape(1, GMLP_DIM).astype(F32)
    return pl.pallas_call(
        _sgu_kernel,
        out_shape=jax.ShapeDtypeStruct((t, GMLP_DIM), BF16),
        grid=(t // rows,),
        in_specs=[
            pl.BlockSpec((rows, GMLP_DIM), lambda i: (i, 0)),
            pl.BlockSpec((rows, GMLP_DIM), lambda i: (i, 1)),
            pl.BlockSpec((1, GMLP_DIM), lambda i: (0, 0)),
            pl.BlockSpec((1, GMLP_DIM), lambda i: (0, 0)),
            pl.BlockSpec((GMLP_GROUPS, CHUNK, CHUNK), lambda i: (0, 0, 0)),
            pl.BlockSpec((CHUNK, GMLP_GROUPS), lambda i: (0, 0)),
        ],
        out_specs=pl.BlockSpec((rows, GMLP_DIM), lambda i: (i, 0)),
        compiler_params=_params("parallel"),
        name="spatial_gating",
    )(z, z, row(ln_g), row(ln_b), w_s.astype(F32), b_s.T.astype(F32))


ROUTER_LANES = V7X_LANES


def _first_argmax(x, lane, width):
    m = jnp.max(x, axis=-1, keepdims=True)
    idx = jnp.min(jnp.where(x == m, lane, width), axis=-1, keepdims=True)
    return m, idx


def _pack_bf16_pairs(x):
    c = x.shape[1] // 2
    lo = pltpu.bitcast(x[:, :c].astype(BF16).astype(F32), jnp.int32)
    hi = pltpu.bitcast(x[:, c:].astype(BF16).astype(F32), jnp.int32)
    return hi | lax.shift_right_logical(lo, 16)


def _unpack_bf16_pairs(p):
    lo = pltpu.bitcast(lax.shift_left(p, 16), F32)
    hi = pltpu.bitcast(p & jnp.int32(-65536), F32)
    return jnp.concatenate([lo, hi], axis=1)


def _router_kernel(x_ref, g_ref, whi_ref, wlo_ref, b_ref, hn_ref, eid_ref, gate_ref, rank_ref, cnt_ref,
                   carry):
    i = pl.program_id(0)
    tm = x_ref.shape[0]

    @pl.when(i == 0)
    def _():
        carry[...] = jnp.zeros_like(carry)

    h = _rmsnorm_rows(x_ref[...], g_ref[...])
    hn_ref[...] = _pack_bf16_pairs(h)
    h_hi = h.astype(BF16)
    h_lo = (h - h_hi.astype(F32)).astype(BF16)
    logits = (jnp.dot(h_hi, whi_ref[...], preferred_element_type=F32)
              + jnp.dot(h_lo, whi_ref[...], preferred_element_type=F32)
              + jnp.dot(h_hi, wlo_ref[...], preferred_element_type=F32) + b_ref[...])
    lane = lax.broadcasted_iota(jnp.int32, logits.shape, 1)
    gl = jnp.where(lane < N_GROUPS, logits, NEG_INF)
    gmax, g_idx = _first_argmax(gl, lane, ROUTER_LANES)
    g_gate = 1.0 / jnp.sum(jnp.exp(gl - gmax), axis=-1, keepdims=True)
    e_lane = lane - N_GROUPS
    in_group = (e_lane >= g_idx * EXPERTS_PER_GROUP) & (e_lane < (g_idx + 1) * EXPERTS_PER_GROUP)
    el = jnp.where(in_group, logits, NEG_INF)
    l1, i1 = _first_argmax(el, lane, ROUTER_LANES)
    el2 = jnp.where(lane == i1, NEG_INF, el)
    l2, i2 = _first_argmax(el2, lane, ROUTER_LANES)
    r = jnp.exp(l2 - l1)
    w1 = g_gate / (1.0 + r)
    w2 = g_gate * r / (1.0 + r)
    e1 = i1 - N_GROUPS
    e2 = i2 - N_GROUPS
    hot1 = lane == e1
    hot2 = lane == e2
    hot = (hot1 | hot2).astype(BF16)
    ri = lax.broadcasted_iota(jnp.int32, (tm, tm), 0)
    ci = lax.broadcasted_iota(jnp.int32, (tm, tm), 1)
    before = (ci < ri).astype(BF16)
    seen = jnp.dot(before, hot, preferred_element_type=F32) + carry[...]
    r1 = jnp.sum(jnp.where(hot1, seen, 0.0), axis=-1, keepdims=True)
    r2 = jnp.sum(jnp.where(hot2, seen, 0.0), axis=-1, keepdims=True)
    carry[...] = carry[...] + jnp.sum(hot.astype(F32), axis=0, keepdims=True)
    eid_ref[...] = jnp.concatenate([e1, e2], axis=1)
    gate_ref[...] = jnp.concatenate([w1, w2], axis=1)
    rank_ref[...] = jnp.concatenate([r1, r2], axis=1).astype(jnp.int32)
    cnt_ref[...] = carry[...].astype(jnp.int32)


def moe_router(x, norm, w_rg, b_rg, w_re, b_re):
    t, d = x.shape
    tm = _tile(t, 256)
    pad = ROUTER_LANES - N_GROUPS - N_EXPERTS
    w = jnp.concatenate([w_rg, w_re, jnp.zeros((d, pad), F32)], axis=1)
    w_hi = w.astype(BF16)
    w_lo = (w - w_hi.astype(F32)).astype(BF16)
    b = jnp.concatenate([b_rg, b_re, jnp.zeros((pad,), F32)]).reshape(1, ROUTER_LANES)
    pair = lambda dt: jax.ShapeDtypeStruct((t, 2), dt)
    pair_spec = pl.BlockSpec((tm, 2), lambda i: (i, 0))
    w_spec = pl.BlockSpec((d, ROUTER_LANES), lambda i: (0, 0))
    return pl.pallas_call(
        _router_kernel,
        out_shape=(jax.ShapeDtypeStruct((t, d // 2), jnp.int32), pair(jnp.int32), pair(F32),
                   pair(jnp.int32), jax.ShapeDtypeStruct((1, ROUTER_LANES), jnp.int32)),
        grid=(t // tm,),
        in_specs=[
            pl.BlockSpec((tm, d), lambda i: (i, 0)),
            pl.BlockSpec((1, d), lambda i: (0, 0)),
            w_spec, w_spec,
            pl.BlockSpec((1, ROUTER_LANES), lambda i: (0, 0)),
        ],
        out_specs=(pl.BlockSpec((tm, d // 2), lambda i: (i, 0)), pair_spec, pair_spec, pair_spec,
                   pl.BlockSpec((1, ROUTER_LANES), lambda i: (0, 0))),
        scratch_shapes=[pltpu.VMEM((1, ROUTER_LANES), F32)],
        compiler_params=_params("arbitrary"),
        name="moe_router",
    )(x, norm.reshape(1, d), w_hi, w_lo, b)


def _row_copy(src_hbm, dst, src_row, dst_row, sem):
    return pltpu.make_async_copy(src_hbm.at[pl.ds(src_row, 1)], dst.at[pl.ds(dst_row, 1)], sem)


def _experts_kernel(te_ref, nxt_ref, na_ref, tok_ref, hn_hbm, wg_hbm, wu_hbm, wd_hbm, ys_ref,
                    hbuf, stage_g, stage_u, stage_d, wg_b, wu_b, wd_b, gsem, wsem):
    i = pl.program_id(0)
    tm = ys_ref.shape[0]
    n_active = na_ref[0]
    e = te_ref[i]
    run_start = (i == 0) | (e != te_ref[jnp.maximum(i - 1, 0)])

    def weight_copies(expert):
        return (pltpu.make_async_copy(wg_hbm.at[expert], stage_g, wsem.at[0]),
                pltpu.make_async_copy(wu_hbm.at[expert], stage_u, wsem.at[1]),
                pltpu.make_async_copy(wd_hbm.at[expert], stage_d, wsem.at[2]))

    def gather(tile, slot):
        def body(r, carry):
            _row_copy(hn_hbm, hbuf.at[slot], tok_ref[tile * tm + r], r, gsem.at[slot]).start()
            return carry
        lax.fori_loop(0, tm, body, 0, unroll=8)

    @pl.when((i == 0) & (n_active > 0))
    def _():
        for c in weight_copies(e):
            c.start()
        gather(0, 0)

    @pl.when((i < n_active) & run_start)
    def _():
        for c in weight_copies(e):
            c.wait()
        wg_b[...] = stage_g[...].astype(BF16)
        wu_b[...] = stage_u[...].astype(BF16)
        wd_b[...] = stage_d[...].astype(BF16)
        nxt = nxt_ref[e]

        @pl.when(nxt >= 0)
        def _():
            for c in weight_copies(nxt):
                c.start()

    @pl.when(i + 1 < n_active)
    def _():
        gather(i + 1, (i + 1) % 2)

    @pl.when(i < n_active)
    def _():
        slot = i % 2
        pltpu.make_async_copy(hn_hbm.at[pl.ds(0, tm)], hbuf.at[slot], gsem.at[slot]).wait()
        h = _unpack_bf16_pairs(hbuf[slot]).astype(BF16)
        g = jnp.dot(h, wg_b[...], preferred_element_type=F32)
        u = jnp.dot(h, wu_b[...], preferred_element_type=F32)
        act = (g * _sigmoid(g) * u).astype(BF16)
        ys_ref[...] = _pack_bf16_pairs(jnp.dot(act, wd_b[...], preferred_element_type=F32))

    @pl.when(i >= n_active)
    def _():
        ys_ref[...] = jnp.zeros_like(ys_ref)


def moe_experts(hn, tok_of_slot, tile_expert, next_expert, n_active, w_gate, w_up, w_down, tm):
    dp = hn.shape[1]
    d = w_gate.shape[1]
    n_slots = tok_of_slot.shape[0]
    any_spec = pl.BlockSpec(memory_space=pl.ANY)
    return pl.pallas_call(
        _experts_kernel,
        out_shape=jax.ShapeDtypeStruct((n_slots, dp), jnp.int32),
        grid_spec=pltpu.PrefetchScalarGridSpec(
            num_scalar_prefetch=4,
            grid=(n_slots // tm,),
            in_specs=[any_spec, any_spec, any_spec, any_spec],
            out_specs=pl.BlockSpec((tm, dp), lambda i, *_: (i, 0)),
            scratch_shapes=[
                pltpu.VMEM((2, tm, dp), jnp.int32),
                pltpu.VMEM((d, D_EXPERT), F32), pltpu.VMEM((d, D_EXPERT), F32),
                pltpu.VMEM((D_EXPERT, d), F32),
                pltpu.VMEM((d, D_EXPERT), BF16), pltpu.VMEM((d, D_EXPERT), BF16),
                pltpu.VMEM((D_EXPERT, d), BF16),
                pltpu.SemaphoreType.DMA((2,)), pltpu.SemaphoreType.DMA((3,)),
            ],
        ),
        compiler_params=_params("arbitrary"),
        name="moe_experts",
    )(tile_expert, next_expert, n_active, tok_of_slot, hn, w_gate, w_up, w_down)


def _combine_kernel(dest_ref, x_ref, gate_ref, ys_hbm, *rest, final_norm):
    if final_norm:
        fg_ref, o_ref, ybuf, sems = rest
    else:
        o_ref, ybuf, sems = rest
    i = pl.program_id(0)
    tc = x_ref.shape[0]

    def gather(step, slot):
        def body(k, carry):
            tok = step * tc + k
            _row_copy(ys_hbm, ybuf.at[slot, 0], dest_ref[2 * tok], k, sems.at[slot]).start()
            _row_copy(ys_hbm, ybuf.at[slot, 1], dest_ref[2 * tok + 1], k, sems.at[slot]).start()
            return carry
        lax.fori_loop(0, tc, body, 0, unroll=4)

    @pl.when(i == 0)
    def _():
        gather(0, 0)

    @pl.when(i + 1 < pl.num_programs(0))
    def _():
        gather(i + 1, (i + 1) % 2)

    slot = i % 2
    for k in range(2):
        pltpu.make_async_copy(ys_hbm.at[pl.ds(0, tc)], ybuf.at[slot, k], sems.at[slot]).wait()
    gate = gate_ref[...]
    y = (x_ref[...] + gate[:, 0:1] * _unpack_bf16_pairs(ybuf[slot, 0])
         + gate[:, 1:2] * _unpack_bf16_pairs(ybuf[slot, 1]))
    if final_norm:
        y = _rmsnorm_rows(y, fg_ref[...])
    o_ref[...] = y


def moe_combine(x, gates, dest, ys, final_g=None):
    t, d = x.shape
    tc = _tile(t, 256)
    final_norm = final_g is not None
    in_specs = [
        pl.BlockSpec((tc, d), lambda i, dst: (i, 0)),
        pl.BlockSpec((tc, 2), lambda i, dst: (i, 0)),
        pl.BlockSpec(memory_space=pl.ANY),
    ]
    args = [x, gates, ys]
    if final_norm:
        in_specs.append(pl.BlockSpec((1, d), lambda i, dst: (0, 0)))
        args.append(final_g.reshape(1, d))
    return pl.pallas_call(
        functools.partial(_combine_kernel, final_norm=final_norm),
        out_shape=jax.ShapeDtypeStruct((t, d), F32),
        grid_spec=pltpu.PrefetchScalarGridSpec(
            num_scalar_prefetch=1,
            grid=(t // tc,),
            in_specs=in_specs,
            out_specs=pl.BlockSpec((tc, d), lambda i, dst: (i, 0)),
            scratch_shapes=[pltpu.VMEM((2, 2, tc, ys.shape[1]), jnp.int32),
                            pltpu.SemaphoreType.DMA((2,))],
        ),
        compiler_params=_params("arbitrary"),
        name="moe_combine",
    )(dest, *args)


def hierarchical_moe(x, norm, w_rg, b_rg, w_re, b_re, w_gate, w_up, w_down, final_g=None):
    t, d = x.shape
    tm = _tile(t, 256)
    n_slots = 2 * t + N_EXPERTS * tm
    hn, eid, gates, rank, counts = moe_router(x, norm, w_rg, b_rg, w_re, b_re)
    counts = counts[0, :N_EXPERTS]
    padded = (counts + tm - 1) // tm * tm
    ends = jnp.cumsum(padded)
    dest = ((ends - padded)[eid] + rank).reshape(2 * t).astype(jnp.int32)
    n_active = (ends[-1] // tm).astype(jnp.int32)
    tile_row = jnp.minimum(jnp.arange(n_slots // tm, dtype=jnp.int32), n_active - 1) * tm
    tile_expert = jnp.sum(tile_row[:, None] >= ends[None, :], axis=1).astype(jnp.int32)
    tok_of_slot = jnp.zeros((n_slots,), jnp.int32).at[dest].set(
        jnp.arange(2 * t, dtype=jnp.int32) // 2, unique_indices=True)
    ids = jnp.arange(N_EXPERTS, dtype=jnp.int32)
    owners = jnp.where(padded > 0, ids, N_EXPERTS)
    later = jnp.min(jnp.where(ids[None, :] > ids[:, None], owners[None, :], N_EXPERTS), axis=1)
    next_expert = jnp.where(later < N_EXPERTS, later, -1).astype(jnp.int32)
    ys = moe_experts(hn, tok_of_slot, tile_expert, next_expert, n_active.reshape(1),
                     w_gate, w_up, w_down, tm)
    return moe_combine(x, gates, dest, ys, final_g)


def kernel(x, positions, l0_norm_mix, l0_w_in, l0_sinks, l0_conv_w, l0_conv_b, l0_ln_g, l0_ln_b, l0_w_out, l0_norm_ffn, l0_w_rg, l0_b_rg, l0_w_re, l0_b_re, l0_w_gate, l0_w_up, l0_w_down, l1_norm_mix, l1_w_in, l1_ln_g, l1_ln_b, l1_w_s, l1_b_s, l1_w_out, l1_norm_ffn, l1_w_rg, l1_b_rg, l1_w_re, l1_b_re, l1_w_gate, l1_w_up, l1_w_down, final_norm):
    b, s, d = x.shape
    t = b * s
    x = x.reshape(t, d)
    bf = lambda w: w.astype(BF16)

    proj = rms_matmul(x, l0_norm_mix, bf(l0_w_in), name="l0_in_proj")
    attn = attention(proj, positions, l0_sinks, s)
    conv = conformer_conv(proj, l0_conv_w, l0_conv_b, l0_ln_g, l0_ln_b, s)
    w_out0 = bf(l0_w_out)
    x = matmul_res([attn, conv], [w_out0[:A_Q], w_out0[A_Q:]], x, name="l0_out_proj")
    x = hierarchical_moe(x, l0_norm_ffn, l0_w_rg, l0_b_rg, l0_w_re, l0_b_re,
                         l0_w_gate, l0_w_up, l0_w_down)
    z = rms_matmul(x, l1_norm_mix, bf(l1_w_in), act="gelu", name="l1_in_proj")
    gated = spatial_gating(z, l1_ln_g, l1_ln_b, l1_w_s, l1_b_s)
    x = matmul_res([gated], [bf(l1_w_out)], x, name="l1_out_proj")
    x = hierarchical_moe(x, l1_norm_ffn, l1_w_rg, l1_b_rg, l1_w_re, l1_b_re,
                         l1_w_gate, l1_w_up, l1_w_down, final_g=final_norm)
    return x.reshape(b, s, d)
```

```python
import functools

import numpy as np
import jax
import jax.numpy as jnp
from jax import lax
from jax.experimental import pallas as pl
from jax.experimental.pallas import tpu as pltpu

D_MODEL = 4096
HEAD_DIM = 64
N_Q_HEADS = 32
N_KV_HEADS = 8
Q_PER_KV = N_Q_HEADS // N_KV_HEADS
BLOCK = 128
ROPE_THETA = 500000.0
ROPE_DIM = HEAD_DIM // 4
ROPE_HALF = ROPE_DIM // 2
A_Q = N_Q_HEADS * HEAD_DIM
A_KV = N_KV_HEADS * HEAD_DIM
CONV_CH = D_MODEL // 2
CONV_WIDTH = 31
CONV_HALO = 32
IN_EVEN = A_Q + 2 * A_KV + 2 * CONV_CH
CHUNK = 128
GMLP_DIM = D_MODEL
GMLP_GROUPS = 16
GMLP_GROUP_DIM = GMLP_DIM // GMLP_GROUPS
N_GROUPS = 4
EXPERTS_PER_GROUP = 8
N_EXPERTS = N_GROUPS * EXPERTS_PER_GROUP
D_EXPERT = 512
EPS = 1e-5
NEG_INF = -1e30

V7X_LANES = 128
V7X_SUBLANES = 8
V7X_VMEM_BYTES = 64 * 1024 * 1024
VMEM_LIMIT = 56 * 1024 * 1024

F32 = jnp.float32
BF16 = jnp.bfloat16


def _tile(n, want):
    t = min(n, want)
    while n % t:
        t //= 2
    return t


def _params(*sem):
    return pltpu.CompilerParams(dimension_semantics=sem, vmem_limit_bytes=VMEM_LIMIT)


def _sigmoid(x):
    return 1.0 / (1.0 + jnp.exp(-x))


def _gelu_tanh(x):
    c = np.float32(np.sqrt(2.0 / np.pi))
    return x * (0.5 * (1.0 + jnp.tanh(c * (x + 0.044715 * (x * x * x)))))


def _rmsnorm_rows(x, g):
    ms = jnp.mean(x * x, axis=-1, keepdims=True)
    return x * lax.rsqrt(ms + EPS) * g


def _rms_matmul_kernel(x_ref, g_ref, w_ref, o_ref, hn_ref, *, act):
    @pl.when(pl.program_id(1) == 0)
    def _():
        hn_ref[...] = _rmsnorm_rows(x_ref[...], g_ref[...]).astype(hn_ref.dtype)

    acc = jnp.dot(hn_ref[...], w_ref[...], preferred_element_type=F32)
    if act == "gelu":
        acc = _gelu_tanh(acc)
    o_ref[...] = acc.astype(o_ref.dtype)


def rms_matmul(x, g, w, *, act=None, out_dtype=BF16, name):
    t, d = x.shape
    n = w.shape[1]
    tm = _tile(t, 512)
    tn = _tile(n, 1024)
    return pl.pallas_call(
        functools.partial(_rms_matmul_kernel, act=act),
        out_shape=jax.ShapeDtypeStruct((t, n), out_dtype),
        grid=(t // tm, n // tn),
        in_specs=[
            pl.BlockSpec((tm, d), lambda i, j: (i, 0)),
            pl.BlockSpec((1, d), lambda i, j: (0, 0)),
            pl.BlockSpec((d, tn), lambda i, j: (0, j)),
        ],
        out_specs=pl.BlockSpec((tm, tn), lambda i, j: (i, j)),
        scratch_shapes=[pltpu.VMEM((tm, d), BF16)],
        compiler_params=_params("parallel", "arbitrary"),
        name=name,
    )(x, g.reshape(1, d), w)


def _rope_tables(pos_ref, invf_ref):
    ang = pos_ref[...].astype(F32) * invf_ref[...]
    c = jnp.cos(ang)
    s = jnp.sin(ang)
    lane = lax.broadcasted_iota(jnp.int32, ang.shape, 1) % HEAD_DIM
    s_lo = jnp.where(lane < ROPE_HALF, -s, 0.0)
    s_hi = jnp.where((lane >= ROPE_HALF) & (lane < ROPE_DIM), s, 0.0)
    return c, s_lo, s_hi


def _apply_rope(x, tables):
    c, s_lo, s_hi = tables
    w = x.shape[1]
    reps = w // c.shape[1]
    c, s_lo, s_hi = (jnp.concatenate([t] * reps, axis=1) for t in (c, s_lo, s_hi))
    return (x * c + pltpu.roll(x, w - ROPE_HALF, 1) * s_lo + pltpu.roll(x, ROPE_HALF, 1) * s_hi)


def _attn_kernel(sink_ref, pos_c_ref, pos_p_ref, invf_ref, q_ref, kc_ref, kp_ref, vc_ref, vp_ref,
                 o_ref, *, blocks_per_seq):
    i = pl.program_id(0)
    not_first = (i % blocks_per_seq) > 0
    tab_c = _rope_tables(pos_c_ref, invf_ref)
    tab_p = _rope_tables(pos_p_ref, invf_ref)
    scale = HEAD_DIM ** -0.5
    q = _apply_rope(q_ref[...].astype(F32), tab_c) * scale
    k = jnp.concatenate([_apply_rope(kp_ref[...].astype(F32), tab_p),
                         _apply_rope(kc_ref[...].astype(F32), tab_c)], axis=0)
    v = jnp.concatenate([vp_ref[...], vc_ref[...]], axis=0).astype(F32)

    qi = lax.broadcasted_iota(jnp.int32, (Q_PER_KV * BLOCK, 2 * BLOCK), 0) % BLOCK + BLOCK
    kj = lax.broadcasted_iota(jnp.int32, (Q_PER_KV * BLOCK, 2 * BLOCK), 1)
    valid = (kj <= qi) & (qi - kj < BLOCK) & (not_first | (kj >= BLOCK))

    col = V7X_LANES
    low_q = lax.broadcasted_iota(jnp.int32, (BLOCK, col), 1) < HEAD_DIM
    low_kv = lax.broadcasted_iota(jnp.int32, (2 * BLOCK, col), 1) < HEAD_DIM
    ones = jnp.ones((2 * BLOCK, col), BF16)
    outs = []
    for c in range(A_KV // col):
        kcol = k[:, c * col:(c + 1) * col]
        vcol = v[:, c * col:(c + 1) * col]
        kswap = pltpu.roll(kcol, HEAD_DIM, 1)
        vswap = pltpu.roll(vcol, HEAD_DIM, 1)
        for half in range(col // HEAD_DIM):
            g = c * (col // HEAD_DIM) + half
            if half == 0:
                kk, vv = jnp.where(low_kv, kcol, kswap), jnp.where(low_kv, vcol, vswap)
            else:
                kk, vv = jnp.where(low_kv, kswap, kcol), jnp.where(low_kv, vswap, vcol)
            vv_ones = jnp.concatenate([vv.astype(BF16), ones], axis=1)
            qa = q[:, (2 * g) * col:(2 * g + 1) * col]
            qb = q[:, (2 * g + 1) * col:(2 * g + 2) * col]
            qs = jnp.concatenate([jnp.where(low_q, qa, 0.0), jnp.where(low_q, 0.0, qa),
                                  jnp.where(low_q, qb, 0.0), jnp.where(low_q, 0.0, qb)],
                                 axis=0).astype(BF16)
            s = lax.dot_general(qs, kk.astype(BF16), (((1,), (1,)), ((), ())),
                                preferred_element_type=F32)
            s = jnp.where(valid, s, NEG_INF)
            sink = jnp.concatenate(
                [jnp.full((BLOCK, 1), sink_ref[g * Q_PER_KV + h], F32) for h in range(Q_PER_KV)], axis=0)
            m = jnp.maximum(jnp.max(s, axis=-1, keepdims=True), sink)
            p = jnp.exp(s - m).astype(BF16)
            o = jnp.dot(p, vv_ones, preferred_element_type=F32)
            o = o[:, :col] / (o[:, col:] + jnp.exp(sink - m))
            outs.append(jnp.where(low_q, o[0:BLOCK], o[BLOCK:2 * BLOCK]))
            outs.append(jnp.where(low_q, o[2 * BLOCK:3 * BLOCK], o[3 * BLOCK:4 * BLOCK]))
    o_ref[...] = jnp.concatenate(outs, axis=1).astype(o_ref.dtype)


def attention(proj, positions, sinks, seq_len):
    t = proj.shape[0]
    nb = t // BLOCK
    invf = ROPE_THETA ** (-np.arange(0, ROPE_DIM, 2, dtype=np.float32) / ROPE_DIM)
    lane = np.arange(V7X_LANES) % HEAD_DIM
    invf_row = np.where(lane < ROPE_DIM, invf[lane % ROPE_HALF], 0.0).astype(np.float32)
    pos = positions.reshape(t, 1)
    kcol = A_Q // A_KV
    prev = lambda i, s: (jnp.maximum(i - 1, 0), 0)
    return pl.pallas_call(
        functools.partial(_attn_kernel, blocks_per_seq=seq_len // BLOCK),
        out_shape=jax.ShapeDtypeStruct((t, A_Q), BF16),
        grid_spec=pltpu.PrefetchScalarGridSpec(
            num_scalar_prefetch=1,
            grid=(nb,),
            in_specs=[
                pl.BlockSpec((BLOCK, 1), lambda i, s: (i, 0)),
                pl.BlockSpec((BLOCK, 1), prev),
                pl.BlockSpec((1, V7X_LANES), lambda i, s: (0, 0)),
                pl.BlockSpec((BLOCK, A_Q), lambda i, s: (i, 0)),
                pl.BlockSpec((BLOCK, A_KV), lambda i, s: (i, kcol)),
                pl.BlockSpec((BLOCK, A_KV), lambda i, s: (jnp.maximum(i - 1, 0), kcol)),
                pl.BlockSpec((BLOCK, A_KV), lambda i, s: (i, kcol + 1)),
                pl.BlockSpec((BLOCK, A_KV), lambda i, s: (jnp.maximum(i - 1, 0), kcol + 1)),
            ],
            out_specs=pl.BlockSpec((BLOCK, A_Q), lambda i, s: (i, 0)),
        ),
        compiler_params=_params("parallel"),
        name="swa_attention",
    )(sinks.astype(F32), pos, pos, jnp.asarray(invf_row).reshape(1, V7X_LANES),
      proj, proj, proj, proj, proj)


def _conv_kernel(a0_ref, a1_ref, g0_ref, g1_ref, w_ref, cb_ref, lg_ref, lb_ref, o_ref,
                 hbuf, ybuf, hsh, *, steps_per_seq):
    i = pl.program_id(0)
    ts = o_ref.shape[0]
    half = CONV_CH // 2

    @pl.when(i % steps_per_seq == 0)
    def _():
        hbuf[0:CONV_HALO, :] = jnp.zeros((CONV_HALO, CONV_CH), F32)

    @pl.when(i % steps_per_seq != 0)
    def _():
        hbuf[0:CONV_HALO, :] = hbuf[ts:ts + CONV_HALO, :]

    for c, (a_ref, g_ref) in enumerate(((a0_ref, g0_ref), (a1_ref, g1_ref))):
        hbuf[CONV_HALO:CONV_HALO + ts, c * half:(c + 1) * half] = (
            a_ref[...].astype(F32) * _sigmoid(g_ref[...].astype(F32)))

    sub = V7X_SUBLANES
    first_tap = CONV_HALO - (CONV_WIDTH - 1)
    h_all = hbuf[...]
    for s in range(1, sub):
        hsh[s - 1] = pltpu.roll(h_all, ts + CONV_HALO - s, 0)

    rows = 2 * sub
    lanes = CONV_CH // 4

    def body(r, carry):
        base = pl.multiple_of(r * rows, rows)
        for c in range(CONV_CH // lanes):
            csl = slice(c * lanes, (c + 1) * lanes)
            acc = [jnp.broadcast_to(cb_ref[:, csl], (sub, lanes)) for _ in range(rows // sub)]
            for j in range(CONV_WIDTH):
                off = first_tap + j
                w = w_ref[j * sub:(j + 1) * sub, csl]
                for b in range(rows // sub):
                    start = base + (off // sub + b) * sub
                    if off % sub == 0:
                        tap = hbuf[pl.ds(start, sub), csl]
                    else:
                        tap = hsh[off % sub - 1, pl.ds(start, sub), csl]
                    acc[b] = acc[b] + tap * w
            for b in range(rows // sub):
                ybuf[pl.ds(base + b * sub, sub), csl] = acc[b]
        return carry

    lax.fori_loop(0, ts // rows, body, 0)

    y = ybuf[...]
    mu = jnp.mean(y, axis=-1, keepdims=True)
    yc = y - mu
    var = jnp.mean(yc * yc, axis=-1, keepdims=True)
    z = yc * lax.rsqrt(var + EPS) * lg_ref[...] + lb_ref[...]
    o_ref[...] = (z * _sigmoid(z)).astype(o_ref.dtype)


def conformer_conv(proj, conv_w, conv_b, ln_g, ln_b, seq_len):
    t = proj.shape[0]
    ts = _tile(seq_len, 256)
    half = CONV_CH // 2
    col0 = (A_Q + 2 * A_KV) // half
    row = lambda v: v.reshape(1, CONV_CH).astype(F32)
    full = lambda shape: pl.BlockSpec(shape, lambda i: (0, 0))
    return pl.pallas_call(
        functools.partial(_conv_kernel, steps_per_seq=seq_len // ts),
        out_shape=jax.ShapeDtypeStruct((t, CONV_CH), BF16),
        grid=(t // ts,),
        in_specs=[
            pl.BlockSpec((ts, half), lambda i: (i, col0)),
            pl.BlockSpec((ts, half), lambda i: (i, col0 + 1)),
            pl.BlockSpec((ts, half), lambda i: (i, col0 + 2)),
            pl.BlockSpec((ts, half), lambda i: (i, col0 + 3)),
            full((CONV_WIDTH * V7X_SUBLANES, CONV_CH)),
            full((1, CONV_CH)), full((1, CONV_CH)), full((1, CONV_CH)),
        ],
        out_specs=pl.BlockSpec((ts, CONV_CH), lambda i: (i, 0)),
        scratch_shapes=[pltpu.VMEM((ts + CONV_HALO, CONV_CH), F32), pltpu.VMEM((ts, CONV_CH), F32),
                        pltpu.VMEM((V7X_SUBLANES - 1, ts + CONV_HALO, CONV_CH), F32)],
        compiler_params=_params("arbitrary"),
        name="conformer_conv",
    )(proj, proj, proj, proj, jnp.repeat(conv_w.astype(F32), V7X_SUBLANES, axis=0),
      row(conv_b), row(ln_g), row(ln_b))


def _matmul_res_kernel(*refs, n_a):
    a_refs, w_refs = refs[:n_a], refs[n_a:2 * n_a]
    r_ref, o_ref = refs[2 * n_a], refs[2 * n_a + 1]
    acc = r_ref[...]
    for a_ref, w_ref in zip(a_refs, w_refs):
        acc = acc + jnp.dot(a_ref[...], w_ref[...], preferred_element_type=F32)
    o_ref[...] = acc


def matmul_res(a_list, w_list, res, *, name):
    t, n = res.shape
    tm = _tile(t, 1024)
    tn = _tile(n, 1024)
    n_a = len(a_list)
    in_specs = [pl.BlockSpec((tm, a.shape[1]), lambda i, j: (i, 0)) for a in a_list]
    in_specs += [pl.BlockSpec((w.shape[0], tn), lambda i, j: (0, j)) for w in w_list]
    in_specs += [pl.BlockSpec((tm, tn), lambda i, j: (i, j))]
    return pl.pallas_call(
        functools.partial(_matmul_res_kernel, n_a=n_a),
        out_shape=jax.ShapeDtypeStruct((t, n), F32),
        grid=(t // tm, n // tn),
        in_specs=in_specs,
        out_specs=pl.BlockSpec((tm, tn), lambda i, j: (i, j)),
        compiler_params=_params("parallel", "arbitrary"),
        name=name,
    )(*a_list, *w_list, res)


def _sgu_kernel(u_ref, v_ref, lg_ref, lb_ref, ws_ref, bs_ref, o_ref):
    rows = o_ref.shape[0]
    v = v_ref[...].astype(F32)
    mu = jnp.mean(v, axis=-1, keepdims=True)
    vc = v - mu
    var = jnp.mean(vc * vc, axis=-1, keepdims=True)
    vn = (vc * lax.rsqrt(var + EPS) * lg_ref[...] + lb_ref[...]).astype(BF16)
    ti = lax.broadcasted_iota(jnp.int32, (CHUNK, CHUNK), 0)
    si = lax.broadcasted_iota(jnp.int32, (CHUNK, CHUNK), 1)
    causal = si <= ti
    for c in range(rows // CHUNK):
        rsl = slice(c * CHUNK, (c + 1) * CHUNK)
        for g in range(GMLP_GROUPS):
            csl = slice(g * GMLP_GROUP_DIM, (g + 1) * GMLP_GROUP_DIM)
            ws = jnp.where(causal, ws_ref[g], 0.0).astype(BF16)
            sv = jnp.dot(ws, vn[rsl, csl], preferred_element_type=F32) + bs_ref[:, g:g + 1]
            o_ref[rsl, csl] = (u_ref[rsl, csl].astype(F32) * sv).astype(o_ref.dtype)


def spatial_gating(z, ln_g, ln_b, w_s, b_s):
    t = z.shape[0]
    rows = _tile(t, 2 * CHUNK)
    row = lambda v: v.reshape(1, GMLP_DIM).astype(F32)
    return pl.pallas_call(
        _sgu_kernel,
        out_shape=jax.ShapeDtypeStruct((t, GMLP_DIM), BF16),
        grid=(t // rows,),
        in_specs=[
            pl.BlockSpec((rows, GMLP_DIM), lambda i: (i, 0)),
            pl.BlockSpec((rows, GMLP_DIM), lambda i: (i, 1)),
            pl.BlockSpec((1, GMLP_DIM), lambda i: (0, 0)),
            pl.BlockSpec((1, GMLP_DIM), lambda i: (0, 0)),
            pl.BlockSpec((GMLP_GROUPS, CHUNK, CHUNK), lambda i: (0, 0, 0)),
            pl.BlockSpec((CHUNK, GMLP_GROUPS), lambda i: (0, 0)),
        ],
        out_specs=pl.BlockSpec((rows, GMLP_DIM), lambda i: (i, 0)),
        compiler_params=_params("parallel"),
        name="spatial_gating",
    )(z, z, row(ln_g), row(ln_b), w_s.astype(F32), b_s.T.astype(F32))


ROUTER_LANES = V7X_LANES


def _first_argmax(x, lane, width):
    m = jnp.max(x, axis=-1, keepdims=True)
    idx = jnp.min(jnp.where(x == m, lane, width), axis=-1, keepdims=True)
    return m, idx


def _pack_bf16_pairs(x):
    c = x.shape[1] // 2
    lo = pltpu.bitcast(x[:, :c].astype(BF16).astype(F32), jnp.int32)
    hi = pltpu.bitcast(x[:, c:].astype(BF16).astype(F32), jnp.int32)
    return hi | lax.shift_right_logical(lo, 16)


def _unpack_bf16_pairs(p):
    lo = pltpu.bitcast(lax.shift_left(p, 16), F32)
    hi = pltpu.bitcast(p & jnp.int32(-65536), F32)
    return jnp.concatenate([lo, hi], axis=1)


def _router_kernel(x_ref, g_ref, whi_ref, wlo_ref, b_ref, hn_ref, eid_ref, gate_ref, rank_ref, cnt_ref,
                   carry):
    i = pl.program_id(0)
    tm = x_ref.shape[0]

    @pl.when(i == 0)
    def _():
        carry[...] = jnp.zeros_like(carry)

    h = _rmsnorm_rows(x_ref[...], g_ref[...])
    hn_ref[...] = _pack_bf16_pairs(h)
    h_hi = h.astype(BF16)
    h_lo = (h - h_hi.astype(F32)).astype(BF16)
    logits = (jnp.dot(h_hi, whi_ref[...], preferred_element_type=F32)
              + jnp.dot(h_lo, whi_ref[...], preferred_element_type=F32)
              + jnp.dot(h_hi, wlo_ref[...], preferred_element_type=F32) + b_ref[...])
    lane = lax.broadcasted_iota(jnp.int32, logits.shape, 1)
    gl = jnp.where(lane < N_GROUPS, logits, NEG_INF)
    gmax, g_idx = _first_argmax(gl, lane, ROUTER_LANES)
    g_gate = 1.0 / jnp.sum(jnp.exp(gl - gmax), axis=-1, keepdims=True)
    e_lane = lane - N_GROUPS
    in_group = (e_lane >= g_idx * EXPERTS_PER_GROUP) & (e_lane < (g_idx + 1) * EXPERTS_PER_GROUP)
    el = jnp.where(in_group, logits, NEG_INF)
    l1, i1 = _first_argmax(el, lane, ROUTER_LANES)
    el2 = jnp.where(lane == i1, NEG_INF, el)
    l2, i2 = _first_argmax(el2, lane, ROUTER_LANES)
    r = jnp.exp(l2 - l1)
    w1 = g_gate / (1.0 + r)
    w2 = g_gate * r / (1.0 + r)
    e1 = i1 - N_GROUPS
    e2 = i2 - N_GROUPS
    hot1 = lane == e1
    hot2 = lane == e2
    hot = (hot1 | hot2).astype(BF16)
    ri = lax.broadcasted_iota(jnp.int32, (tm, tm), 0)
    ci = lax.broadcasted_iota(jnp.int32, (tm, tm), 1)
    before = (ci < ri).astype(BF16)
    seen = jnp.dot(before, hot, preferred_element_type=F32) + carry[...]
    r1 = jnp.sum(jnp.where(hot1, seen, 0.0), axis=-1, keepdims=True)
    r2 = jnp.sum(jnp.where(hot2, seen, 0.0), axis=-1, keepdims=True)
    carry[...] = carry[...] + jnp.sum(hot.astype(F32), axis=0, keepdims=True)
    eid_ref[...] = jnp.concatenate([e1, e2], axis=1)
    gate_ref[...] = jnp.concatenate([w1, w2], axis=1)
    rank_ref[...] = jnp.concatenate([r1, r2], axis=1).astype(jnp.int32)
    cnt_ref[...] = carry[...].astype(jnp.int32)


def moe_router(x, norm, w_rg, b_rg, w_re, b_re):
    t, d = x.shape
    tm = _tile(t, 256)
    pad = ROUTER_LANES - N_GROUPS - N_EXPERTS
    w = jnp.concatenate([w_rg, w_re, jnp.zeros((d, pad), F32)], axis=1)
    w_hi = w.astype(BF16)
    w_lo = (w - w_hi.astype(F32)).astype(BF16)
    b = jnp.concatenate([b_rg, b_re, jnp.zeros((pad,), F32)]).reshape(1, ROUTER_LANES)
    pair = lambda dt: jax.ShapeDtypeStruct((t, 2), dt)
    pair_spec = pl.BlockSpec((tm, 2), lambda i: (i, 0))
    w_spec = pl.BlockSpec((d, ROUTER_LANES), lambda i: (0, 0))
    return pl.pallas_call(
        _router_kernel,
        out_shape=(jax.ShapeDtypeStruct((t, d // 2), jnp.int32), pair(jnp.int32), pair(F32),
                   pair(jnp.int32), jax.ShapeDtypeStruct((1, ROUTER_LANES), jnp.int32)),
        grid=(t // tm,),
        in_specs=[
            pl.BlockSpec((tm, d), lambda i: (i, 0)),
            pl.BlockSpec((1, d), lambda i: (0, 0)),
            w_spec, w_spec,
            pl.BlockSpec((1, ROUTER_LANES), lambda i: (0, 0)),
        ],
        out_specs=(pl.BlockSpec((tm, d // 2), lambda i: (i, 0)), pair_spec, pair_spec, pair_spec,
                   pl.BlockSpec((1, ROUTER_LANES), lambda i: (0, 0))),
        scratch_shapes=[pltpu.VMEM((1, ROUTER_LANES), F32)],
        compiler_params=_params("arbitrary"),
        name="moe_router",
    )(x, norm.reshape(1, d), w_hi, w_lo, b)


def _row_copy(src_hbm, dst, src_row, dst_row, sem):
    return pltpu.make_async_copy(src_hbm.at[pl.ds(src_row, 1)], dst.at[pl.ds(dst_row, 1)], sem)


def _experts_kernel(te_ref, nxt_ref, na_ref, tok_ref, hn_hbm, wg_hbm, wu_hbm, wd_hbm, ys_ref,
                    hbuf, stage_g, stage_u, stage_d, wg_b, wu_b, wd_b, gsem, wsem):
    i = pl.program_id(0)
    tm = ys_ref.shape[0]
    n_active = na_ref[0]
    e = te_ref[i]
    run_start = (i == 0) | (e != te_ref[jnp.maximum(i - 1, 0)])

    def weight_copies(expert):
        return (pltpu.make_async_copy(wg_hbm.at[expert], stage_g, wsem.at[0]),
                pltpu.make_async_copy(wu_hbm.at[expert], stage_u, wsem.at[1]),
                pltpu.make_async_copy(wd_hbm.at[expert], stage_d, wsem.at[2]))

    def gather(tile, slot):
        def body(r, carry):
            _row_copy(hn_hbm, hbuf.at[slot], tok_ref[tile * tm + r], r, gsem.at[slot]).start()
            return carry
        lax.fori_loop(0, tm, body, 0, unroll=8)

    @pl.when((i == 0) & (n_active > 0))
    def _():
        for c in weight_copies(e):
            c.start()
        gather(0, 0)

    @pl.when((i < n_active) & run_start)
    def _():
        for c in weight_copies(e):
            c.wait()
        wg_b[...] = stage_g[...].astype(BF16)
        wu_b[...] = stage_u[...].astype(BF16)
        wd_b[...] = stage_d[...].astype(BF16)
        nxt = nxt_ref[e]

        @pl.when(nxt >= 0)
        def _():
            for c in weight_copies(nxt):
                c.start()

    def wait_gather(slot):
        pltpu.make_async_copy(hn_hbm.at[pl.ds(0, tm)], hbuf.at[slot], gsem.at[slot]).wait()

    @pl.when(i < n_active)
    def _():
        slot = i % 2
        wait_gather(slot)
        h = _unpack_bf16_pairs(hbuf[slot]).astype(BF16)
        for r in range(tm):
            _row_copy(hn_hbm, hbuf.at[1 - slot], tok_ref[(i + 1) * tm + r], r, gsem.at[1 - slot]).start()
        g = jnp.dot(h, wg_b[...], preferred_element_type=F32)
        u = jnp.dot(h, wu_b[...], preferred_element_type=F32)
        act = (g * _sigmoid(g) * u).astype(BF16)
        ys_ref[...] = _pack_bf16_pairs(jnp.dot(act, wd_b[...], preferred_element_type=F32))

    @pl.when((i == n_active) & (i > 0))
    def _():
        wait_gather(i % 2)

    @pl.when(i >= n_active)
    def _():
        ys_ref[...] = jnp.zeros_like(ys_ref)


def moe_experts(hn, tok_of_slot, tile_expert, next_expert, n_active, w_gate, w_up, w_down, tm):
    dp = hn.shape[1]
    d = w_gate.shape[1]
    n_slots = tok_of_slot.shape[0]
    any_spec = pl.BlockSpec(memory_space=pl.ANY)
    return pl.pallas_call(
        _experts_kernel,
        out_shape=jax.ShapeDtypeStruct((n_slots, dp), jnp.int32),
        grid_spec=pltpu.PrefetchScalarGridSpec(
            num_scalar_prefetch=4,
            grid=(n_slots // tm,),
            in_specs=[any_spec, any_spec, any_spec, any_spec],
            out_specs=pl.BlockSpec((tm, dp), lambda i, *_: (i, 0)),
            scratch_shapes=[
                pltpu.VMEM((2, tm, dp), jnp.int32),
                pltpu.VMEM((d, D_EXPERT), F32), pltpu.VMEM((d, D_EXPERT), F32),
                pltpu.VMEM((D_EXPERT, d), F32),
                pltpu.VMEM((d, D_EXPERT), BF16), pltpu.VMEM((d, D_EXPERT), BF16),
                pltpu.VMEM((D_EXPERT, d), BF16),
                pltpu.SemaphoreType.DMA((2,)), pltpu.SemaphoreType.DMA((3,)),
            ],
        ),
        compiler_params=_params("arbitrary"),
        name="moe_experts",
    )(tile_expert, next_expert, n_active, tok_of_slot, hn, w_gate, w_up, w_down)


def _combine_kernel(dest_ref, x_ref, gate_ref, ys_hbm, *rest, final_norm):
    if final_norm:
        fg_ref, o_ref, ybuf, sems = rest
    else:
        o_ref, ybuf, sems = rest
    i = pl.program_id(0)
    tc = x_ref.shape[0]

    last = pl.num_programs(0) - 1

    def start_row(step, slot, k):
        tok = step * tc + k
        _row_copy(ys_hbm, ybuf.at[slot, 0], dest_ref[2 * tok], k, sems.at[slot]).start()
        _row_copy(ys_hbm, ybuf.at[slot, 1], dest_ref[2 * tok + 1], k, sems.at[slot]).start()

    def wait_rows(slot):
        for c in range(2):
            pltpu.make_async_copy(ys_hbm.at[pl.ds(0, tc)], ybuf.at[slot, c], sems.at[slot]).wait()

    @pl.when(i == 0)
    def _():
        lax.fori_loop(0, tc, lambda k, c: (start_row(0, 0, k), c)[1], 0, unroll=4)

    slot = i % 2
    wait_rows(slot)
    for k in range(tc):
        start_row(jnp.minimum(i + 1, last), 1 - slot, k)
    gate = gate_ref[...]
    y = (x_ref[...] + gate[:, 0:1] * _unpack_bf16_pairs(ybuf[slot, 0])
         + gate[:, 1:2] * _unpack_bf16_pairs(ybuf[slot, 1]))
    if final_norm:
        y = _rmsnorm_rows(y, fg_ref[...])
    o_ref[...] = y

    @pl.when(i == last)
    def _():
        wait_rows(1 - slot)


def moe_combine(x, gates, dest, ys, final_g=None):
    t, d = x.shape
    tc = _tile(t, 256)
    final_norm = final_g is not None
    in_specs = [
        pl.BlockSpec((tc, d), lambda i, dst: (i, 0)),
        pl.BlockSpec((tc, 2), lambda i, dst: (i, 0)),
        pl.BlockSpec(memory_space=pl.ANY),
    ]
    args = [x, gates, ys]
    if final_norm:
        in_specs.append(pl.BlockSpec((1, d), lambda i, dst: (0, 0)))
        args.append(final_g.reshape(1, d))
    return pl.pallas_call(
        functools.partial(_combine_kernel, final_norm=final_norm),
        out_shape=jax.ShapeDtypeStruct((t, d), F32),
        grid_spec=pltpu.PrefetchScalarGridSpec(
            num_scalar_prefetch=1,
            grid=(t // tc,),
            in_specs=in_specs,
            out_specs=pl.BlockSpec((tc, d), lambda i, dst: (i, 0)),
            scratch_shapes=[pltpu.VMEM((2, 2, tc, ys.shape[1]), jnp.int32),
                            pltpu.SemaphoreType.DMA((2,))],
        ),
        compiler_params=_params("arbitrary"),
        name="moe_combine",
    )(dest, *args)


def hierarchical_moe(x, norm, w_rg, b_rg, w_re, b_re, w_gate, w_up, w_down, final_g=None):
    t, d = x.shape
    tm = _tile(t, 256)
    n_slots = 2 * t + N_EXPERTS * tm
    hn, eid, gates, rank, counts = moe_router(x, norm, w_rg, b_rg, w_re, b_re)
    counts = counts[0, :N_EXPERTS]
    padded = (counts + tm - 1) // tm * tm
    ends = jnp.cumsum(padded)
    dest = ((ends - padded)[eid] + rank).reshape(2 * t).astype(jnp.int32)
    n_active = (ends[-1] // tm).astype(jnp.int32)
    tile_row = jnp.minimum(jnp.arange(n_slots // tm, dtype=jnp.int32), n_active - 1) * tm
    tile_expert = jnp.sum(tile_row[:, None] >= ends[None, :], axis=1).astype(jnp.int32)
    tok_of_slot = jnp.zeros((n_slots,), jnp.int32).at[dest].set(
        jnp.arange(2 * t, dtype=jnp.int32) // 2, unique_indices=True)
    ids = jnp.arange(N_EXPERTS, dtype=jnp.int32)
    owners = jnp.where(padded > 0, ids, N_EXPERTS)
    later = jnp.min(jnp.where(ids[None, :] > ids[:, None], owners[None, :], N_EXPERTS), axis=1)
    next_expert = jnp.where(later < N_EXPERTS, later, -1).astype(jnp.int32)
    ys = moe_experts(hn, tok_of_slot, tile_expert, next_expert, n_active.reshape(1),
                     w_gate, w_up, w_down, tm)
    return moe_combine(x, gates, dest, ys, final_g)


def kernel(x, positions, l0_norm_mix, l0_w_in, l0_sinks, l0_conv_w, l0_conv_b, l0_ln_g, l0_ln_b, l0_w_out, l0_norm_ffn, l0_w_rg, l0_b_rg, l0_w_re, l0_b_re, l0_w_gate, l0_w_up, l0_w_down, l1_norm_mix, l1_w_in, l1_ln_g, l1_ln_b, l1_w_s, l1_b_s, l1_w_out, l1_norm_ffn, l1_w_rg, l1_b_rg, l1_w_re, l1_b_re, l1_w_gate, l1_w_up, l1_w_down, final_norm):
    b, s, d = x.shape
    t = b * s
    x = x.reshape(t, d)
    bf = lambda w: w.astype(BF16)

    proj = rms_matmul(x, l0_norm_mix, bf(l0_w_in), name="l0_in_proj")
    attn = attention(proj, positions, l0_sinks, s)
    conv = conformer_conv(proj, l0_conv_w, l0_conv_b, l0_ln_g, l0_ln_b, s)
    w_out0 = bf(l0_w_out)
    x = matmul_res([attn, conv], [w_out0[:A_Q], w_out0[A_Q:]], x, name="l0_out_proj")
    x = hierarchical_moe(x, l0_norm_ffn, l0_w_rg, l0_b_rg, l0_w_re, l0_b_re,
                         l0_w_gate, l0_w_up, l0_w_down)
    z = rms_matmul(x, l1_norm_mix, bf(l1_w_in), act="gelu", name="l1_in_proj")
    gated = spatial_gating(z, l1_ln_g, l1_ln_b, l1_w_s, l1_b_s)
    x = matmul_res([gated], [bf(l1_w_out)], x, name="l1_out_proj")
    x = hierarchical_moe(x, l1_norm_ffn, l1_w_rg, l1_b_rg, l1_w_re, l1_b_re,
                         l1_w_gate, l1_w_up, l1_w_down, final_g=final_norm)
    return x.reshape(b, s, d)
```

```python
import functools

import numpy as np
import jax
import jax.numpy as jnp
from jax import lax
from jax.experimental import pallas as pl
from jax.experimental.pallas import tpu as pltpu

D_MODEL = 4096
HEAD_DIM = 64
N_Q_HEADS = 32
N_KV_HEADS = 8
Q_PER_KV = N_Q_HEADS // N_KV_HEADS
BLOCK = 128
ROPE_THETA = 500000.0
ROPE_DIM = HEAD_DIM // 4
ROPE_HALF = ROPE_DIM // 2
A_Q = N_Q_HEADS * HEAD_DIM
A_KV = N_KV_HEADS * HEAD_DIM
CONV_CH = D_MODEL // 2
CONV_WIDTH = 31
CONV_HALO = 32
IN_EVEN = A_Q + 2 * A_KV + 2 * CONV_CH
CHUNK = 128
GMLP_DIM = D_MODEL
GMLP_GROUPS = 16
GMLP_GROUP_DIM = GMLP_DIM // GMLP_GROUPS
N_GROUPS = 4
EXPERTS_PER_GROUP = 8
N_EXPERTS = N_GROUPS * EXPERTS_PER_GROUP
D_EXPERT = 512
EPS = 1e-5
NEG_INF = -1e30
LOG2_E = float(np.log2(np.e))

V7X_LANES = 128
V7X_SUBLANES = 8
V7X_VMEM_BYTES = 64 * 1024 * 1024
VMEM_LIMIT = 56 * 1024 * 1024

F32 = jnp.float32
BF16 = jnp.bfloat16


def _tile(n, want):
    t = min(n, want)
    while n % t:
        t //= 2
    return t


def _params(*sem):
    return pltpu.CompilerParams(dimension_semantics=sem, vmem_limit_bytes=VMEM_LIMIT)


def _sigmoid(x):
    return 1.0 / (1.0 + jnp.exp(-x))


def _gelu_tanh(x):
    c = np.float32(np.sqrt(2.0 / np.pi))
    return x * (0.5 * (1.0 + jnp.tanh(c * (x + 0.044715 * (x * x * x)))))


def _rmsnorm_rows(x, g):
    ms = jnp.mean(x * x, axis=-1, keepdims=True)
    return x * lax.rsqrt(ms + EPS) * g


def _rms_matmul_kernel(x_ref, g_ref, w_ref, o_ref, hn_ref, *, act):
    @pl.when(pl.program_id(1) == 0)
    def _():
        hn_ref[...] = _rmsnorm_rows(x_ref[...], g_ref[...]).astype(hn_ref.dtype)

    acc = jnp.dot(hn_ref[...], w_ref[...], preferred_element_type=F32)
    if act == "gelu":
        acc = _gelu_tanh(acc)
    o_ref[...] = acc.astype(o_ref.dtype)


def rms_matmul(x, g, w, *, act=None, out_dtype=BF16, name):
    t, d = x.shape
    n = w.shape[1]
    tm = _tile(t, 512)
    tn = _tile(n, 1024)
    return pl.pallas_call(
        functools.partial(_rms_matmul_kernel, act=act),
        out_shape=jax.ShapeDtypeStruct((t, n), out_dtype),
        grid=(t // tm, n // tn),
        in_specs=[
            pl.BlockSpec((tm, d), lambda i, j: (i, 0)),
            pl.BlockSpec((1, d), lambda i, j: (0, 0)),
            pl.BlockSpec((d, tn), lambda i, j: (0, j)),
        ],
        out_specs=pl.BlockSpec((tm, tn), lambda i, j: (i, j)),
        scratch_shapes=[pltpu.VMEM((tm, d), BF16)],
        compiler_params=_params("parallel", "arbitrary"),
        name=name,
    )(x, g.reshape(1, d), w)


def _rope_tables(pos_ref, invf_ref):
    ang = pos_ref[...].astype(F32) * invf_ref[...]
    c = jnp.cos(ang)
    s = jnp.sin(ang)
    lane = lax.broadcasted_iota(jnp.int32, ang.shape, 1) % HEAD_DIM
    s_lo = jnp.where(lane < ROPE_HALF, -s, 0.0)
    s_hi = jnp.where((lane >= ROPE_HALF) & (lane < ROPE_DIM), s, 0.0)
    return c, s_lo, s_hi


def _apply_rope(x, tables):
    c, s_lo, s_hi = tables
    w = x.shape[1]
    reps = w // c.shape[1]
    c, s_lo, s_hi = (jnp.concatenate([t] * reps, axis=1) for t in (c, s_lo, s_hi))
    return (x * c + pltpu.roll(x, w - ROPE_HALF, 1) * s_lo + pltpu.roll(x, ROPE_HALF, 1) * s_hi)


def _attn_kernel(sink_ref, pos_c_ref, pos_p_ref, invf_ref, q_ref, kc_ref, kp_ref, vc_ref, vp_ref,
                 o_ref, *, blocks_per_seq):
    i = pl.program_id(0)
    not_first = (i % blocks_per_seq) > 0
    tab_c = _rope_tables(pos_c_ref, invf_ref)
    tab_p = _rope_tables(pos_p_ref, invf_ref)
    scale = HEAD_DIM ** -0.5 * LOG2_E
    q = _apply_rope(q_ref[...].astype(F32), tab_c) * scale
    k = jnp.concatenate([_apply_rope(kp_ref[...].astype(F32), tab_p),
                         _apply_rope(kc_ref[...].astype(F32), tab_c)], axis=0)
    v = jnp.concatenate([vp_ref[...], vc_ref[...]], axis=0).astype(F32)

    qi = lax.broadcasted_iota(jnp.int32, (Q_PER_KV * BLOCK, 2 * BLOCK), 0) % BLOCK + BLOCK
    kj = lax.broadcasted_iota(jnp.int32, (Q_PER_KV * BLOCK, 2 * BLOCK), 1)
    valid = (kj <= qi) & (qi - kj < BLOCK) & (not_first | (kj >= BLOCK))

    col = V7X_LANES
    low_q = lax.broadcasted_iota(jnp.int32, (BLOCK, col), 1) < HEAD_DIM
    low_kv = lax.broadcasted_iota(jnp.int32, (2 * BLOCK, col), 1) < HEAD_DIM
    ones = jnp.ones((2 * BLOCK, col), BF16)
    outs = []
    for c in range(A_KV // col):
        kcol = k[:, c * col:(c + 1) * col]
        vcol = v[:, c * col:(c + 1) * col]
        kswap = pltpu.roll(kcol, HEAD_DIM, 1)
        vswap = pltpu.roll(vcol, HEAD_DIM, 1)
        for half in range(col // HEAD_DIM):
            g = c * (col // HEAD_DIM) + half
            if half == 0:
                kk, vv = jnp.where(low_kv, kcol, kswap), jnp.where(low_kv, vcol, vswap)
            else:
                kk, vv = jnp.where(low_kv, kswap, kcol), jnp.where(low_kv, vswap, vcol)
            vv_ones = jnp.concatenate([vv.astype(BF16), ones], axis=1)
            qa = q[:, (2 * g) * col:(2 * g + 1) * col]
            qb = q[:, (2 * g + 1) * col:(2 * g + 2) * col]
            qs = jnp.concatenate([jnp.where(low_q, qa, 0.0), jnp.where(low_q, 0.0, qa),
                                  jnp.where(low_q, qb, 0.0), jnp.where(low_q, 0.0, qb)],
                                 axis=0).astype(BF16)
            s = lax.dot_general(qs, kk.astype(BF16), (((1,), (1,)), ((), ())),
                                preferred_element_type=F32)
            s = jnp.where(valid, s, NEG_INF)
            sink = jnp.concatenate(
                [jnp.full((BLOCK, 1), sink_ref[g * Q_PER_KV + h] * LOG2_E, F32)
                 for h in range(Q_PER_KV)], axis=0)
            m = jnp.maximum(jnp.max(s, axis=-1, keepdims=True), sink)
            p = jnp.exp2(s - m).astype(BF16)
            o = jnp.dot(p, vv_ones, preferred_element_type=F32)
            o = o[:, :col] / (o[:, col:] + jnp.exp2(sink - m))
            outs.append(jnp.where(low_q, o[0:BLOCK], o[BLOCK:2 * BLOCK]))
            outs.append(jnp.where(low_q, o[2 * BLOCK:3 * BLOCK], o[3 * BLOCK:4 * BLOCK]))
    o_ref[...] = jnp.concatenate(outs, axis=1).astype(o_ref.dtype)


def attention(proj, positions, sinks, seq_len):
    t = proj.shape[0]
    nb = t // BLOCK
    invf = ROPE_THETA ** (-np.arange(0, ROPE_DIM, 2, dtype=np.float32) / ROPE_DIM)
    lane = np.arange(V7X_LANES) % HEAD_DIM
    invf_row = np.where(lane < ROPE_DIM, invf[lane % ROPE_HALF], 0.0).astype(np.float32)
    pos = positions.reshape(t, 1)
    kcol = A_Q // A_KV
    prev = lambda i, s: (jnp.maximum(i - 1, 0), 0)
    return pl.pallas_call(
        functools.partial(_attn_kernel, blocks_per_seq=seq_len // BLOCK),
        out_shape=jax.ShapeDtypeStruct((t, A_Q), BF16),
        grid_spec=pltpu.PrefetchScalarGridSpec(
            num_scalar_prefetch=1,
            grid=(nb,),
            in_specs=[
                pl.BlockSpec((BLOCK, 1), lambda i, s: (i, 0)),
                pl.BlockSpec((BLOCK, 1), prev),
                pl.BlockSpec((1, V7X_LANES), lambda i, s: (0, 0)),
                pl.BlockSpec((BLOCK, A_Q), lambda i, s: (i, 0)),
                pl.BlockSpec((BLOCK, A_KV), lambda i, s: (i, kcol)),
                pl.BlockSpec((BLOCK, A_KV), lambda i, s: (jnp.maximum(i - 1, 0), kcol)),
                pl.BlockSpec((BLOCK, A_KV), lambda i, s: (i, kcol + 1)),
                pl.BlockSpec((BLOCK, A_KV), lambda i, s: (jnp.maximum(i - 1, 0), kcol + 1)),
            ],
            out_specs=pl.BlockSpec((BLOCK, A_Q), lambda i, s: (i, 0)),
        ),
        compiler_params=_params("parallel"),
        name="swa_attention",
    )(sinks.astype(F32), pos, pos, jnp.asarray(invf_row).reshape(1, V7X_LANES),
      proj, proj, proj, proj, proj)


def _conv_kernel(a0_ref, a1_ref, g0_ref, g1_ref, w_ref, cb_ref, lg_ref, lb_ref, o_ref,
                 hbuf, ybuf, hsh, *, steps_per_seq):
    i = pl.program_id(0)
    ts = o_ref.shape[0]
    half = CONV_CH // 2

    @pl.when(i % steps_per_seq == 0)
    def _():
        hbuf[0:CONV_HALO, :] = jnp.zeros((CONV_HALO, CONV_CH), F32)

    @pl.when(i % steps_per_seq != 0)
    def _():
        hbuf[0:CONV_HALO, :] = hbuf[ts:ts + CONV_HALO, :]

    for c, (a_ref, g_ref) in enumerate(((a0_ref, g0_ref), (a1_ref, g1_ref))):
        hbuf[CONV_HALO:CONV_HALO + ts, c * half:(c + 1) * half] = (
            a_ref[...].astype(F32) * _sigmoid(g_ref[...].astype(F32)))

    sub = V7X_SUBLANES
    first_tap = CONV_HALO - (CONV_WIDTH - 1)
    h_all = hbuf[...]
    for s in range(1, sub):
        hsh[s - 1] = pltpu.roll(h_all, ts + CONV_HALO - s, 0)

    rows = 2 * sub
    lanes = CONV_CH // 4

    def body(r, carry):
        base = pl.multiple_of(r * rows, rows)
        for c in range(CONV_CH // lanes):
            csl = slice(c * lanes, (c + 1) * lanes)
            acc = [jnp.broadcast_to(cb_ref[:, csl], (sub, lanes)) for _ in range(rows // sub)]
            for j in range(CONV_WIDTH):
                off = first_tap + j
                w = w_ref[j * sub:(j + 1) * sub, csl]
                for b in range(rows // sub):
                    start = base + (off // sub + b) * sub
                    if off % sub == 0:
                        tap = hbuf[pl.ds(start, sub), csl]
                    else:
                        tap = hsh[off % sub - 1, pl.ds(start, sub), csl]
                    acc[b] = acc[b] + tap * w
            for b in range(rows // sub):
                ybuf[pl.ds(base + b * sub, sub), csl] = acc[b]
        return carry

    lax.fori_loop(0, ts // rows, body, 0)

    y = ybuf[...]
    mu = jnp.mean(y, axis=-1, keepdims=True)
    yc = y - mu
    var = jnp.mean(yc * yc, axis=-1, keepdims=True)
    z = yc * lax.rsqrt(var + EPS) * lg_ref[...] + lb_ref[...]
    o_ref[...] = (z * _sigmoid(z)).astype(o_ref.dtype)


def conformer_conv(proj, conv_w, conv_b, ln_g, ln_b, seq_len):
    t = proj.shape[0]
    ts = _tile(seq_len, 256)
    half = CONV_CH // 2
    col0 = (A_Q + 2 * A_KV) // half
    row = lambda v: v.reshape(1, CONV_CH).astype(F32)
    full = lambda shape: pl.BlockSpec(shape, lambda i: (0, 0))
    return pl.pallas_call(
        functools.partial(_conv_kernel, steps_per_seq=seq_len // ts),
        out_shape=jax.ShapeDtypeStruct((t, CONV_CH), BF16),
        grid=(t // ts,),
        in_specs=[
            pl.BlockSpec((ts, half), lambda i: (i, col0)),
            pl.BlockSpec((ts, half), lambda i: (i, col0 + 1)),
            pl.BlockSpec((ts, half), lambda i: (i, col0 + 2)),
            pl.BlockSpec((ts, half), lambda i: (i, col0 + 3)),
            full((CONV_WIDTH * V7X_SUBLANES, CONV_CH)),
            full((1, CONV_CH)), full((1, CONV_CH)), full((1, CONV_CH)),
        ],
        out_specs=pl.BlockSpec((ts, CONV_CH), lambda i: (i, 0)),
        scratch_shapes=[pltpu.VMEM((ts + CONV_HALO, CONV_CH), F32), pltpu.VMEM((ts, CONV_CH), F32),
                        pltpu.VMEM((V7X_SUBLANES - 1, ts + CONV_HALO, CONV_CH), F32)],
        compiler_params=_params("arbitrary"),
        name="conformer_conv",
    )(proj, proj, proj, proj, jnp.repeat(conv_w.astype(F32), V7X_SUBLANES, axis=0),
      row(conv_b), row(ln_g), row(ln_b))


def _matmul_res_kernel(*refs, n_a):
    a_refs, w_refs = refs[:n_a], refs[n_a:2 * n_a]
    r_ref, o_ref = refs[2 * n_a], refs[2 * n_a + 1]
    acc = r_ref[...]
    for a_ref, w_ref in zip(a_refs, w_refs):
        acc = acc + jnp.dot(a_ref[...], w_ref[...], preferred_element_type=F32)
    o_ref[...] = acc


def matmul_res(a_list, w_list, res, *, name):
    t, n = res.shape
    tm = _tile(t, 1024)
    tn = _tile(n, 1024)
    n_a = len(a_list)
    in_specs = [pl.BlockSpec((tm, a.shape[1]), lambda i, j: (i, 0)) for a in a_list]
    in_specs += [pl.BlockSpec((w.shape[0], tn), lambda i, j: (0, j)) for w in w_list]
    in_specs += [pl.BlockSpec((tm, tn), lambda i, j: (i, j))]
    return pl.pallas_call(
        functools.partial(_matmul_res_kernel, n_a=n_a),
        out_shape=jax.ShapeDtypeStruct((t, n), F32),
        grid=(t // tm, n // tn),
        in_specs=in_specs,
        out_specs=pl.BlockSpec((tm, tn), lambda i, j: (i, j)),
        compiler_params=_params("parallel", "arbitrary"),
        name=name,
    )(*a_list, *w_list, res)


def _sgu_kernel(u_ref, v_ref, lg_ref, lb_ref, ws_ref, bs_ref, o_ref):
    rows = o_ref.shape[0]
    v = v_ref[...].astype(F32)
    mu = jnp.mean(v, axis=-1, keepdims=True)
    vc = v - mu
    var = jnp.mean(vc * vc, axis=-1, keepdims=True)
    vn = (vc * lax.rsqrt(var + EPS) * lg_ref[...] + lb_ref[...]).astype(BF16)
    ti = lax.broadcasted_iota(jnp.int32, (CHUNK, CHUNK), 0)
    si = lax.broadcasted_iota(jnp.int32, (CHUNK, CHUNK), 1)
    causal = si <= ti
    for c in range(rows // CHUNK):
        rsl = slice(c * CHUNK, (c + 1) * CHUNK)
        for g in range(GMLP_GROUPS):
            csl = slice(g * GMLP_GROUP_DIM, (g + 1) * GMLP_GROUP_DIM)
            ws = jnp.where(causal, ws_ref[g], 0.0).astype(BF16)
            sv = jnp.dot(ws, vn[rsl, csl], preferred_element_type=F32) + bs_ref[:, g:g + 1]
            o_ref[rsl, csl] = (u_ref[rsl, csl].astype(F32) * sv).astype(o_ref.dtype)


def spatial_gating(z, ln_g, ln_b, w_s, b_s):
    t = z.shape[0]
    rows = _tile(t, 2 * CHUNK)
    row = lambda v: v.reshape(1, GMLP_DIM).astype(F32)
    return pl.pallas_call(
        _sgu_kernel,
        out_shape=jax.ShapeDtypeStruct((t, GMLP_DIM), BF16),
        grid=(t // rows,),
        in_specs=[
            pl.BlockSpec((rows, GMLP_DIM), lambda i: (i, 0)),
            pl.BlockSpec((rows, GMLP_DIM), lambda i: (i, 1)),
            pl.BlockSpec((1, GMLP_DIM), lambda i: (0, 0)),
            pl.BlockSpec((1, GMLP_DIM), lambda i: (0, 0)),
            pl.BlockSpec((GMLP_GROUPS, CHUNK, CHUNK), lambda i: (0, 0, 0)),
            pl.BlockSpec((CHUNK, GMLP_GROUPS), lambda i: (0, 0)),
        ],
        out_specs=pl.BlockSpec((rows, GMLP_DIM), lambda i: (i, 0)),
        compiler_params=_params("parallel"),
        name="spatial_gating",
    )(z, z, row(ln_g), row(ln_b), w_s.astype(F32), b_s.T.astype(F32))


ROUTER_LANES = V7X_LANES


def _first_argmax(x, lane, width):
    m = jnp.max(x, axis=-1, keepdims=True)
    idx = jnp.min(jnp.where(x == m, lane, width), axis=-1, keepdims=True)
    return m, idx


def _pack_bf16_pairs(x):
    c = x.shape[1] // 2
    lo = pltpu.bitcast(x[:, :c].astype(BF16).astype(F32), jnp.int32)
    hi = pltpu.bitcast(x[:, c:].astype(BF16).astype(F32), jnp.int32)
    return hi | lax.shift_right_logical(lo, 16)


def _unpack_bf16_pairs(p):
    lo = pltpu.bitcast(lax.shift_left(p, 16), F32)
    hi = pltpu.bitcast(p & jnp.int32(-65536), F32)
    return jnp.concatenate([lo, hi], axis=1)


def _router_kernel(x_ref, g_ref, whi_ref, wlo_ref, b_ref, hn_ref, eid_ref, gate_ref, rank_ref, cnt_ref,
                   carry):
    i = pl.program_id(0)
    tm = x_ref.shape[0]

    @pl.when(i == 0)
    def _():
        carry[...] = jnp.zeros_like(carry)

    h = _rmsnorm_rows(x_ref[...], g_ref[...])
    hn_ref[...] = _pack_bf16_pairs(h)
    h_hi = h.astype(BF16)
    h_lo = (h - h_hi.astype(F32)).astype(BF16)
    logits = (jnp.dot(h_hi, whi_ref[...], preferred_element_type=F32)
              + jnp.dot(h_lo, whi_ref[...], preferred_element_type=F32)
              + jnp.dot(h_hi, wlo_ref[...], preferred_element_type=F32) + b_ref[...])
    lane = lax.broadcasted_iota(jnp.int32, logits.shape, 1)
    gl = jnp.where(lane < N_GROUPS, logits, NEG_INF)
    gmax, g_idx = _first_argmax(gl, lane, ROUTER_LANES)
    g_gate = 1.0 / jnp.sum(jnp.exp(gl - gmax), axis=-1, keepdims=True)
    e_lane = lane - N_GROUPS
    in_group = (e_lane >= g_idx * EXPERTS_PER_GROUP) & (e_lane < (g_idx + 1) * EXPERTS_PER_GROUP)
    el = jnp.where(in_group, logits, NEG_INF)
    l1, i1 = _first_argmax(el, lane, ROUTER_LANES)
    el2 = jnp.where(lane == i1, NEG_INF, el)
    l2, i2 = _first_argmax(el2, lane, ROUTER_LANES)
    r = jnp.exp(l2 - l1)
    w1 = g_gate / (1.0 + r)
    w2 = g_gate * r / (1.0 + r)
    e1 = i1 - N_GROUPS
    e2 = i2 - N_GROUPS
    hot1 = lane == e1
    hot2 = lane == e2
    hot = (hot1 | hot2).astype(BF16)
    ri = lax.broadcasted_iota(jnp.int32, (tm, tm), 0)
    ci = lax.broadcasted_iota(jnp.int32, (tm, tm), 1)
    before = (ci < ri).astype(BF16)
    seen = jnp.dot(before, hot, preferred_element_type=F32) + carry[...]
    r1 = jnp.sum(jnp.where(hot1, seen, 0.0), axis=-1, keepdims=True)
    r2 = jnp.sum(jnp.where(hot2, seen, 0.0), axis=-1, keepdims=True)
    carry[...] = carry[...] + jnp.sum(hot.astype(F32), axis=0, keepdims=True)
    eid_ref[...] = jnp.concatenate([e1, e2], axis=1)
    gate_ref[...] = jnp.concatenate([w1, w2], axis=1)
    rank_ref[...] = jnp.concatenate([r1, r2], axis=1).astype(jnp.int32)
    cnt_ref[...] = carry[...].astype(jnp.int32)


def moe_router(x, norm, w_rg, b_rg, w_re, b_re):
    t, d = x.shape
    tm = _tile(t, 256)
    pad = ROUTER_LANES - N_GROUPS - N_EXPERTS
    w = jnp.concatenate([w_rg, w_re, jnp.zeros((d, pad), F32)], axis=1)
    w_hi = w.astype(BF16)
    w_lo = (w - w_hi.astype(F32)).astype(BF16)
    b = jnp.concatenate([b_rg, b_re, jnp.zeros((pad,), F32)]).reshape(1, ROUTER_LANES)
    pair = lambda dt: jax.ShapeDtypeStruct((t, 2), dt)
    pair_spec = pl.BlockSpec((tm, 2), lambda i: (i, 0))
    w_spec = pl.BlockSpec((d, ROUTER_LANES), lambda i: (0, 0))
    return pl.pallas_call(
        _router_kernel,
        out_shape=(jax.ShapeDtypeStruct((t, d // 2), jnp.int32), pair(jnp.int32), pair(F32),
                   pair(jnp.int32), jax.ShapeDtypeStruct((1, ROUTER_LANES), jnp.int32)),
        grid=(t // tm,),
        in_specs=[
            pl.BlockSpec((tm, d), lambda i: (i, 0)),
            pl.BlockSpec((1, d), lambda i: (0, 0)),
            w_spec, w_spec,
            pl.BlockSpec((1, ROUTER_LANES), lambda i: (0, 0)),
        ],
        out_specs=(pl.BlockSpec((tm, d // 2), lambda i: (i, 0)), pair_spec, pair_spec, pair_spec,
                   pl.BlockSpec((1, ROUTER_LANES), lambda i: (0, 0))),
        scratch_shapes=[pltpu.VMEM((1, ROUTER_LANES), F32)],
        compiler_params=_params("arbitrary"),
        name="moe_router",
    )(x, norm.reshape(1, d), w_hi, w_lo, b)


def _row_copy(src_hbm, dst, src_row, dst_row, sem):
    return pltpu.make_async_copy(src_hbm.at[pl.ds(src_row, 1)], dst.at[pl.ds(dst_row, 1)], sem)


def _experts_kernel(te_ref, nxt_ref, na_ref, tok_ref, hn_hbm, wg_hbm, wu_hbm, wd_hbm, ys_ref,
                    hbuf, stage_g, stage_u, stage_d, wg_b, wu_b, wd_b, gsem, wsem):
    i = pl.program_id(0)
    tm = ys_ref.shape[0]
    n_active = na_ref[0]
    e = te_ref[i]
    run_start = (i == 0) | (e != te_ref[jnp.maximum(i - 1, 0)])

    def weight_copies(expert):
        return (pltpu.make_async_copy(wg_hbm.at[expert], stage_g, wsem.at[0]),
                pltpu.make_async_copy(wu_hbm.at[expert], stage_u, wsem.at[1]),
                pltpu.make_async_copy(wd_hbm.at[expert], stage_d, wsem.at[2]))

    def gather(tile, slot):
        def body(r, carry):
            _row_copy(hn_hbm, hbuf.at[slot], tok_ref[tile * tm + r], r, gsem.at[slot]).start()
            return carry
        lax.fori_loop(0, tm, body, 0, unroll=8)

    @pl.when((i == 0) & (n_active > 0))
    def _():
        for c in weight_copies(e):
            c.start(priority=1)
        gather(0, 0)

    @pl.when((i < n_active) & run_start)
    def _():
        for c in weight_copies(e):
            c.wait()
        wg_b[...] = stage_g[...].astype(BF16)
        wu_b[...] = stage_u[...].astype(BF16)
        wd_b[...] = stage_d[...].astype(BF16)
        nxt = nxt_ref[e]

        @pl.when(nxt >= 0)
        def _():
            for c in weight_copies(nxt):
                c.start(priority=1)

    def wait_gather(slot):
        pltpu.make_async_copy(hn_hbm.at[pl.ds(0, tm)], hbuf.at[slot], gsem.at[slot]).wait()

    @pl.when(i < n_active)
    def _():
        slot = i % 2
        wait_gather(slot)
        h = _unpack_bf16_pairs(hbuf[slot]).astype(BF16)
        for r in range(tm):
            _row_copy(hn_hbm, hbuf.at[1 - slot], tok_ref[(i + 1) * tm + r], r, gsem.at[1 - slot]).start()
        g = jnp.dot(h, wg_b[...], preferred_element_type=F32)
        u = jnp.dot(h, wu_b[...], preferred_element_type=F32)
        act = (g * _sigmoid(g) * u).astype(BF16)
        ys_ref[...] = _pack_bf16_pairs(jnp.dot(act, wd_b[...], preferred_element_type=F32))

    @pl.when((i == n_active) & (i > 0))
    def _():
        wait_gather(i % 2)

    @pl.when(i >= n_active)
    def _():
        ys_ref[...] = jnp.zeros_like(ys_ref)


def moe_experts(hn, tok_of_slot, tile_expert, next_expert, n_active, w_gate, w_up, w_down, tm):
    dp = hn.shape[1]
    d = w_gate.shape[1]
    n_slots = tok_of_slot.shape[0]
    any_spec = pl.BlockSpec(memory_space=pl.ANY)
    return pl.pallas_call(
        _experts_kernel,
        out_shape=jax.ShapeDtypeStruct((n_slots, dp), jnp.int32),
        grid_spec=pltpu.PrefetchScalarGridSpec(
            num_scalar_prefetch=4,
            grid=(n_slots // tm,),
            in_specs=[any_spec, any_spec, any_spec, any_spec],
            out_specs=pl.BlockSpec((tm, dp), lambda i, *_: (i, 0)),
            scratch_shapes=[
                pltpu.VMEM((2, tm, dp), jnp.int32),
                pltpu.VMEM((d, D_EXPERT), F32), pltpu.VMEM((d, D_EXPERT), F32),
                pltpu.VMEM((D_EXPERT, d), F32),
                pltpu.VMEM((d, D_EXPERT), BF16), pltpu.VMEM((d, D_EXPERT), BF16),
                pltpu.VMEM((D_EXPERT, d), BF16),
                pltpu.SemaphoreType.DMA((2,)), pltpu.SemaphoreType.DMA((3,)),
            ],
        ),
        compiler_params=_params("arbitrary"),
        name="moe_experts",
    )(tile_expert, next_expert, n_active, tok_of_slot, hn, w_gate, w_up, w_down)


def _combine_kernel(dest_ref, x_ref, gate_ref, ys_hbm, *rest, final_norm):
    if final_norm:
        fg_ref, o_ref, ybuf, sems = rest
    else:
        o_ref, ybuf, sems = rest
    i = pl.program_id(0)
    tc = x_ref.shape[0]

    last = pl.num_programs(0) - 1

    def start_row(step, slot, k):
        tok = step * tc + k
        _row_copy(ys_hbm, ybuf.at[slot, 0], dest_ref[2 * tok], k, sems.at[slot]).start()
        _row_copy(ys_hbm, ybuf.at[slot, 1], dest_ref[2 * tok + 1], k, sems.at[slot]).start()

    def wait_rows(slot):
        for c in range(2):
            pltpu.make_async_copy(ys_hbm.at[pl.ds(0, tc)], ybuf.at[slot, c], sems.at[slot]).wait()

    @pl.when(i == 0)
    def _():
        lax.fori_loop(0, tc, lambda k, c: (start_row(0, 0, k), c)[1], 0, unroll=4)

    slot = i % 2
    wait_rows(slot)
    for k in range(tc):
        start_row(jnp.minimum(i + 1, last), 1 - slot, k)
    gate = gate_ref[...]
    y = (x_ref[...] + gate[:, 0:1] * _unpack_bf16_pairs(ybuf[slot, 0])
         + gate[:, 1:2] * _unpack_bf16_pairs(ybuf[slot, 1]))
    if final_norm:
        y = _rmsnorm_rows(y, fg_ref[...])
    o_ref[...] = y

    @pl.when(i == last)
    def _():
        wait_rows(1 - slot)


def moe_combine(x, gates, dest, ys, final_g=None):
    t, d = x.shape
    tc = _tile(t, 256)
    final_norm = final_g is not None
    in_specs = [
        pl.BlockSpec((tc, d), lambda i, dst: (i, 0)),
        pl.BlockSpec((tc, 2), lambda i, dst: (i, 0)),
        pl.BlockSpec(memory_space=pl.ANY),
    ]
    args = [x, gates, ys]
    if final_norm:
        in_specs.append(pl.BlockSpec((1, d), lambda i, dst: (0, 0)))
        args.append(final_g.reshape(1, d))
    return pl.pallas_call(
        functools.partial(_combine_kernel, final_norm=final_norm),
        out_shape=jax.ShapeDtypeStruct((t, d), F32),
        grid_spec=pltpu.PrefetchScalarGridSpec(
            num_scalar_prefetch=1,
            grid=(t // tc,),
            in_specs=in_specs,
            out_specs=pl.BlockSpec((tc, d), lambda i, dst: (i, 0)),
            scratch_shapes=[pltpu.VMEM((2, 2, tc, ys.shape[1]), jnp.int32),
                            pltpu.SemaphoreType.DMA((2,))],
        ),
        compiler_params=_params("arbitrary"),
        name="moe_combine",
    )(dest, *args)


def hierarchical_moe(x, norm, w_rg, b_rg, w_re, b_re, w_gate, w_up, w_down, final_g=None):
    t, d = x.shape
    tm = _tile(t, 256)
    n_slots = 2 * t + N_EXPERTS * tm
    hn, eid, gates, rank, counts = moe_router(x, norm, w_rg, b_rg, w_re, b_re)
    counts = counts[0, :N_EXPERTS]
    padded = (counts + tm - 1) // tm * tm
    ends = jnp.cumsum(padded)
    dest = ((ends - padded)[eid] + rank).reshape(2 * t).astype(jnp.int32)
    n_active = (ends[-1] // tm).astype(jnp.int32)
    tile_row = jnp.minimum(jnp.arange(n_slots // tm, dtype=jnp.int32), n_active - 1) * tm
    tile_expert = jnp.sum(tile_row[:, None] >= ends[None, :], axis=1).astype(jnp.int32)
    tok_of_slot = jnp.zeros((n_slots,), jnp.int32).at[dest].set(
        jnp.arange(2 * t, dtype=jnp.int32) // 2, unique_indices=True)
    ids = jnp.arange(N_EXPERTS, dtype=jnp.int32)
    owners = jnp.where(padded > 0, ids, N_EXPERTS)
    later = jnp.min(jnp.where(ids[None, :] > ids[:, None], owners[None, :], N_EXPERTS), axis=1)
    next_expert = jnp.where(later < N_EXPERTS, later, -1).astype(jnp.int32)
    ys = moe_experts(hn, tok_of_slot, tile_expert, next_expert, n_active.reshape(1),
                     w_gate, w_up, w_down, tm)
    return moe_combine(x, gates, dest, ys, final_g)


def kernel(x, positions, l0_norm_mix, l0_w_in, l0_sinks, l0_conv_w, l0_conv_b, l0_ln_g, l0_ln_b, l0_w_out, l0_norm_ffn, l0_w_rg, l0_b_rg, l0_w_re, l0_b_re, l0_w_gate, l0_w_up, l0_w_down, l1_norm_mix, l1_w_in, l1_ln_g, l1_ln_b, l1_w_s, l1_b_s, l1_w_out, l1_norm_ffn, l1_w_rg, l1_b_rg, l1_w_re, l1_b_re, l1_w_gate, l1_w_up, l1_w_down, final_norm):
    b, s, d = x.shape
    t = b * s
    x = x.reshape(t, d)
    bf = lambda w: w.astype(BF16)

    proj = rms_matmul(x, l0_norm_mix, bf(l0_w_in), name="l0_in_proj")
    attn = attention(proj, positions, l0_sinks, s)
    conv = conformer_conv(proj, l0_conv_w, l0_conv_b, l0_ln_g, l0_ln_b, s)
    w_out0 = bf(l0_w_out)
    x = matmul_res([attn, conv], [w_out0[:A_Q], w_out0[A_Q:]], x, name="l0_out_proj")
    x = hierarchical_moe(x, l0_norm_ffn, l0_w_rg, l0_b_rg, l0_w_re, l0_b_re,
                         l0_w_gate, l0_w_up, l0_w_down)
    z = rms_matmul(x, l1_norm_mix, bf(l1_w_in), act="gelu", name="l1_in_proj")
    gated = spatial_gating(z, l1_ln_g, l1_ln_b, l1_w_s, l1_b_s)
    x = matmul_res([gated], [bf(l1_w_out)], x, name="l1_out_proj")
    x = hierarchical_moe(x, l1_norm_ffn, l1_w_rg, l1_b_rg, l1_w_re, l1_b_re,
                         l1_w_gate, l1_w_up, l1_w_down, final_g=final_norm)
    return x.reshape(b, s, d)
```

```python
import functools

import numpy as np
import jax
import jax.numpy as jnp
from jax import lax
from jax.experimental import pallas as pl
from jax.experimental.pallas import tpu as pltpu

D_MODEL = 4096
HEAD_DIM = 64
N_Q_HEADS = 32
N_KV_HEADS = 8
Q_PER_KV = N_Q_HEADS // N_KV_HEADS
BLOCK = 128
ROPE_THETA = 500000.0
ROPE_DIM = HEAD_DIM // 4
ROPE_HALF = ROPE_DIM // 2
A_Q = N_Q_HEADS * HEAD_DIM
A_KV = N_KV_HEADS * HEAD_DIM
CONV_CH = D_MODEL // 2
CONV_WIDTH = 31
CONV_HALO = 32
IN_EVEN = A_Q + 2 * A_KV + 2 * CONV_CH
CHUNK = 128
GMLP_DIM = D_MODEL
GMLP_GROUPS = 16
GMLP_GROUP_DIM = GMLP_DIM // GMLP_GROUPS
N_GROUPS = 4
EXPERTS_PER_GROUP = 8
N_EXPERTS = N_GROUPS * EXPERTS_PER_GROUP
D_EXPERT = 512
EPS = 1e-5
NEG_INF = -1e30
LOG2_E = float(np.log2(np.e))

V7X_LANES = 128
V7X_SUBLANES = 8
V7X_VMEM_BYTES = 64 * 1024 * 1024
VMEM_LIMIT = 56 * 1024 * 1024

F32 = jnp.float32
BF16 = jnp.bfloat16


def _tile(n, want):
    t = min(n, want)
    while n % t:
        t //= 2
    return t


def _params(*sem):
    return pltpu.CompilerParams(dimension_semantics=sem, vmem_limit_bytes=VMEM_LIMIT)


def _sigmoid(x):
    return 1.0 / (1.0 + jnp.exp(-x))


def _gelu_tanh(x):
    c = np.float32(np.sqrt(2.0 / np.pi))
    return x * (0.5 * (1.0 + jnp.tanh(c * (x + 0.044715 * (x * x * x)))))


def _rmsnorm_rows(x, g):
    ms = jnp.mean(x * x, axis=-1, keepdims=True)
    return x * lax.rsqrt(ms + EPS) * g


def _prenorm_kernel(x_ref, g_ref, o_ref):
    o_ref[...] = _rmsnorm_rows(x_ref[...], g_ref[...]).astype(o_ref.dtype)


def prenorm(x, g):
    t, d = x.shape
    tm = _tile(t, 512)
    return pl.pallas_call(
        _prenorm_kernel,
        out_shape=jax.ShapeDtypeStruct((t, d), BF16),
        grid=(t // tm,),
        in_specs=[pl.BlockSpec((tm, d), lambda i: (i, 0)), pl.BlockSpec((1, d), lambda i: (0, 0))],
        out_specs=pl.BlockSpec((tm, d), lambda i: (i, 0)),
        compiler_params=_params("parallel"),
        name="prenorm",
    )(x, g.reshape(1, d))


def _matmul_kernel(*refs, n_a, has_res, act):
    a_refs, w_refs = refs[:n_a], refs[n_a:2 * n_a]
    o_ref = refs[-1]
    acc = None
    for a_ref, w_ref in zip(a_refs, w_refs):
        part = jnp.dot(a_ref[...], w_ref[...], preferred_element_type=F32)
        acc = part if acc is None else acc + part
    if has_res:
        acc = acc + refs[2 * n_a][...]
    if act == "gelu":
        acc = _gelu_tanh(acc)
    o_ref[...] = acc.astype(o_ref.dtype)


def matmul(a_list, w_list, *, res=None, act=None, out_dtype, name):
    t = a_list[0].shape[0]
    n = w_list[0].shape[1]
    tm = _tile(t, 1024)
    tn = _tile(n, 1024)
    n_a = len(a_list)
    in_specs = [pl.BlockSpec((tm, a.shape[1]), lambda i, j: (i, 0)) for a in a_list]
    in_specs += [pl.BlockSpec((w.shape[0], tn), lambda i, j: (0, j)) for w in w_list]
    args = [*a_list, *w_list]
    if res is not None:
        in_specs.append(pl.BlockSpec((tm, tn), lambda i, j: (i, j)))
        args.append(res)
    return pl.pallas_call(
        functools.partial(_matmul_kernel, n_a=n_a, has_res=res is not None, act=act),
        out_shape=jax.ShapeDtypeStruct((t, n), out_dtype),
        grid=(t // tm, n // tn),
        in_specs=in_specs,
        out_specs=pl.BlockSpec((tm, tn), lambda i, j: (i, j)),
        compiler_params=_params("parallel", "arbitrary"),
        name=name,
    )(*args)


def _rope_tables(pos_ref, invf_ref):
    ang = pos_ref[...].astype(F32) * invf_ref[...]
    c = jnp.cos(ang)
    s = jnp.sin(ang)
    lane = lax.broadcasted_iota(jnp.int32, ang.shape, 1) % HEAD_DIM
    s_lo = jnp.where(lane < ROPE_HALF, -s, 0.0)
    s_hi = jnp.where((lane >= ROPE_HALF) & (lane < ROPE_DIM), s, 0.0)
    return c, s_lo, s_hi


def _apply_rope(x, tables):
    c, s_lo, s_hi = tables
    w = x.shape[1]
    reps = w // c.shape[1]
    c, s_lo, s_hi = (jnp.concatenate([t] * reps, axis=1) for t in (c, s_lo, s_hi))
    return (x * c + pltpu.roll(x, w - ROPE_HALF, 1) * s_lo + pltpu.roll(x, ROPE_HALF, 1) * s_hi)


def _attn_kernel(sink_ref, pos_c_ref, pos_p_ref, invf_ref, q_ref, kc_ref, kp_ref, vc_ref, vp_ref,
                 o_ref, *, blocks_per_seq):
    i = pl.program_id(0)
    not_first = (i % blocks_per_seq) > 0
    tab_c = _rope_tables(pos_c_ref, invf_ref)
    tab_p = _rope_tables(pos_p_ref, invf_ref)
    scale = HEAD_DIM ** -0.5 * LOG2_E
    q = _apply_rope(q_ref[...].astype(F32), tab_c) * scale
    k = jnp.concatenate([_apply_rope(kp_ref[...].astype(F32), tab_p),
                         _apply_rope(kc_ref[...].astype(F32), tab_c)], axis=0)
    v = jnp.concatenate([vp_ref[...], vc_ref[...]], axis=0).astype(F32)

    qi = lax.broadcasted_iota(jnp.int32, (Q_PER_KV * BLOCK, 2 * BLOCK), 0) % BLOCK + BLOCK
    kj = lax.broadcasted_iota(jnp.int32, (Q_PER_KV * BLOCK, 2 * BLOCK), 1)
    valid = (kj <= qi) & (qi - kj < BLOCK) & (not_first | (kj >= BLOCK))

    col = V7X_LANES
    low_q = lax.broadcasted_iota(jnp.int32, (BLOCK, col), 1) < HEAD_DIM
    low_kv = lax.broadcasted_iota(jnp.int32, (2 * BLOCK, col), 1) < HEAD_DIM
    ones = jnp.ones((2 * BLOCK, col), BF16)
    outs = []
    for c in range(A_KV // col):
        kcol = k[:, c * col:(c + 1) * col]
        vcol = v[:, c * col:(c + 1) * col]
        kswap = pltpu.roll(kcol, HEAD_DIM, 1)
        vswap = pltpu.roll(vcol, HEAD_DIM, 1)
        for half in range(col // HEAD_DIM):
            g = c * (col // HEAD_DIM) + half
            if half == 0:
                kk, vv = jnp.where(low_kv, kcol, kswap), jnp.where(low_kv, vcol, vswap)
            else:
                kk, vv = jnp.where(low_kv, kswap, kcol), jnp.where(low_kv, vswap, vcol)
            vv_ones = jnp.concatenate([vv.astype(BF16), ones], axis=1)
            qa = q[:, (2 * g) * col:(2 * g + 1) * col]
            qb = q[:, (2 * g + 1) * col:(2 * g + 2) * col]
            qs = jnp.concatenate([jnp.where(low_q, qa, 0.0), jnp.where(low_q, 0.0, qa),
                                  jnp.where(low_q, qb, 0.0), jnp.where(low_q, 0.0, qb)],
                                 axis=0).astype(BF16)
            s = lax.dot_general(qs, kk.astype(BF16), (((1,), (1,)), ((), ())),
                                preferred_element_type=F32)
            s = jnp.where(valid, s, NEG_INF)
            sink = jnp.concatenate(
                [jnp.full((BLOCK, 1), sink_ref[g * Q_PER_KV + h] * LOG2_E, F32)
                 for h in range(Q_PER_KV)], axis=0)
            m = jnp.maximum(jnp.max(s, axis=-1, keepdims=True), sink)
            p = jnp.exp2(s - m).astype(BF16)
            o = jnp.dot(p, vv_ones, preferred_element_type=F32)
            o = o[:, :col] / (o[:, col:] + jnp.exp2(sink - m))
            outs.append(jnp.where(low_q, o[0:BLOCK], o[BLOCK:2 * BLOCK]))
            outs.append(jnp.where(low_q, o[2 * BLOCK:3 * BLOCK], o[3 * BLOCK:4 * BLOCK]))
    o_ref[...] = jnp.concatenate(outs, axis=1).astype(o_ref.dtype)


def attention(proj, positions, sinks, seq_len):
    t = proj.shape[0]
    nb = t // BLOCK
    invf = ROPE_THETA ** (-np.arange(0, ROPE_DIM, 2, dtype=np.float32) / ROPE_DIM)
    lane = np.arange(V7X_LANES) % HEAD_DIM
    invf_row = np.where(lane < ROPE_DIM, invf[lane % ROPE_HALF], 0.0).astype(np.float32)
    pos = positions.reshape(t, 1)
    kcol = A_Q // A_KV
    prev = lambda i, s: (jnp.maximum(i - 1, 0), 0)
    return pl.pallas_call(
        functools.partial(_attn_kernel, blocks_per_seq=seq_len // BLOCK),
        out_shape=jax.ShapeDtypeStruct((t, A_Q), BF16),
        grid_spec=pltpu.PrefetchScalarGridSpec(
            num_scalar_prefetch=1,
            grid=(nb,),
            in_specs=[
                pl.BlockSpec((BLOCK, 1), lambda i, s: (i, 0)),
                pl.BlockSpec((BLOCK, 1), prev),
                pl.BlockSpec((1, V7X_LANES), lambda i, s: (0, 0)),
                pl.BlockSpec((BLOCK, A_Q), lambda i, s: (i, 0)),
                pl.BlockSpec((BLOCK, A_KV), lambda i, s: (i, kcol)),
                pl.BlockSpec((BLOCK, A_KV), lambda i, s: (jnp.maximum(i - 1, 0), kcol)),
                pl.BlockSpec((BLOCK, A_KV), lambda i, s: (i, kcol + 1)),
                pl.BlockSpec((BLOCK, A_KV), lambda i, s: (jnp.maximum(i - 1, 0), kcol + 1)),
            ],
            out_specs=pl.BlockSpec((BLOCK, A_Q), lambda i, s: (i, 0)),
        ),
        compiler_params=_params("parallel"),
        name="swa_attention",
    )(sinks.astype(F32), pos, pos, jnp.asarray(invf_row).reshape(1, V7X_LANES),
      proj, proj, proj, proj, proj)


def _conv_kernel(a0_ref, a1_ref, g0_ref, g1_ref, w_ref, cb_ref, lg_ref, lb_ref, o_ref,
                 hbuf, ybuf, hsh, *, steps_per_seq):
    i = pl.program_id(0)
    ts = o_ref.shape[0]
    half = CONV_CH // 2

    @pl.when(i % steps_per_seq == 0)
    def _():
        hbuf[0:CONV_HALO, :] = jnp.zeros((CONV_HALO, CONV_CH), F32)

    @pl.when(i % steps_per_seq != 0)
    def _():
        hbuf[0:CONV_HALO, :] = hbuf[ts:ts + CONV_HALO, :]

    for c, (a_ref, g_ref) in enumerate(((a0_ref, g0_ref), (a1_ref, g1_ref))):
        hbuf[CONV_HALO:CONV_HALO + ts, c * half:(c + 1) * half] = (
            a_ref[...].astype(F32) * _sigmoid(g_ref[...].astype(F32)))

    sub = V7X_SUBLANES
    first_tap = CONV_HALO - (CONV_WIDTH - 1)
    h_all = hbuf[...]
    for s in range(1, sub):
        hsh[s - 1] = pltpu.roll(h_all, ts + CONV_HALO - s, 0)

    rows = 2 * sub
    lanes = CONV_CH // 4

    def body(r, carry):
        base = pl.multiple_of(r * rows, rows)
        for c in range(CONV_CH // lanes):
            csl = slice(c * lanes, (c + 1) * lanes)
            acc = [jnp.broadcast_to(cb_ref[:, csl], (sub, lanes)) for _ in range(rows // sub)]
            for j in range(CONV_WIDTH):
                off = first_tap + j
                w = w_ref[j * sub:(j + 1) * sub, csl]
                for b in range(rows // sub):
                    start = base + (off // sub + b) * sub
                    if off % sub == 0:
                        tap = hbuf[pl.ds(start, sub), csl]
                    else:
                        tap = hsh[off % sub - 1, pl.ds(start, sub), csl]
                    acc[b] = acc[b] + tap * w
            for b in range(rows // sub):
                ybuf[pl.ds(base + b * sub, sub), csl] = acc[b]
        return carry

    lax.fori_loop(0, ts // rows, body, 0)

    y = ybuf[...]
    mu = jnp.mean(y, axis=-1, keepdims=True)
    yc = y - mu
    var = jnp.mean(yc * yc, axis=-1, keepdims=True)
    z = yc * lax.rsqrt(var + EPS) * lg_ref[...] + lb_ref[...]
    o_ref[...] = (z * _sigmoid(z)).astype(o_ref.dtype)


def conformer_conv(proj, conv_w, conv_b, ln_g, ln_b, seq_len):
    t = proj.shape[0]
    ts = _tile(seq_len, 256)
    half = CONV_CH // 2
    col0 = (A_Q + 2 * A_KV) // half
    row = lambda v: v.reshape(1, CONV_CH).astype(F32)
    full = lambda shape: pl.BlockSpec(shape, lambda i: (0, 0))
    return pl.pallas_call(
        functools.partial(_conv_kernel, steps_per_seq=seq_len // ts),
        out_shape=jax.ShapeDtypeStruct((t, CONV_CH), BF16),
        grid=(t // ts,),
        in_specs=[
            pl.BlockSpec((ts, half), lambda i: (i, col0)),
            pl.BlockSpec((ts, half), lambda i: (i, col0 + 1)),
            pl.BlockSpec((ts, half), lambda i: (i, col0 + 2)),
            pl.BlockSpec((ts, half), lambda i: (i, col0 + 3)),
            full((CONV_WIDTH * V7X_SUBLANES, CONV_CH)),
            full((1, CONV_CH)), full((1, CONV_CH)), full((1, CONV_CH)),
        ],
        out_specs=pl.BlockSpec((ts, CONV_CH), lambda i: (i, 0)),
        scratch_shapes=[pltpu.VMEM((ts + CONV_HALO, CONV_CH), F32), pltpu.VMEM((ts, CONV_CH), F32),
                        pltpu.VMEM((V7X_SUBLANES - 1, ts + CONV_HALO, CONV_CH), F32)],
        compiler_params=_params("arbitrary"),
        name="conformer_conv",
    )(proj, proj, proj, proj, jnp.repeat(conv_w.astype(F32), V7X_SUBLANES, axis=0),
      row(conv_b), row(ln_g), row(ln_b))


def _sgu_kernel(u_ref, v_ref, lg_ref, lb_ref, ws_ref, bs_ref, o_ref):
    rows = o_ref.shape[0]
    v = v_ref[...].astype(F32)
    mu = jnp.mean(v, axis=-1, keepdims=True)
    vc = v - mu
    var = jnp.mean(vc * vc, axis=-1, keepdims=True)
    vn = (vc * lax.rsqrt(var + EPS) * lg_ref[...] + lb_ref[...]).astype(BF16)
    ti = lax.broadcasted_iota(jnp.int32, (CHUNK, CHUNK), 0)
    si = lax.broadcasted_iota(jnp.int32, (CHUNK, CHUNK), 1)
    causal = si <= ti
    for c in range(rows // CHUNK):
        rsl = slice(c * CHUNK, (c + 1) * CHUNK)
        for g in range(GMLP_GROUPS):
            csl = slice(g * GMLP_GROUP_DIM, (g + 1) * GMLP_GROUP_DIM)
            ws = jnp.where(causal, ws_ref[g], 0.0).astype(BF16)
            sv = jnp.dot(ws, vn[rsl, csl], preferred_element_type=F32) + bs_ref[:, g:g + 1]
            o_ref[rsl, csl] = (u_ref[rsl, csl].astype(F32) * sv).astype(o_ref.dtype)


def spatial_gating(z, ln_g, ln_b, w_s, b_s):
    t = z.shape[0]
    rows = _tile(t, 2 * CHUNK)
    row = lambda v: v.reshape(1, GMLP_DIM).astype(F32)
    return pl.pallas_call(
        _sgu_kernel,
        out_shape=jax.ShapeDtypeStruct((t, GMLP_DIM), BF16),
        grid=(t // rows,),
        in_specs=[
            pl.BlockSpec((rows, GMLP_DIM), lambda i: (i, 0)),
            pl.BlockSpec((rows, GMLP_DIM), lambda i: (i, 1)),
            pl.BlockSpec((1, GMLP_DIM), lambda i: (0, 0)),
            pl.BlockSpec((1, GMLP_DIM), lambda i: (0, 0)),
            pl.BlockSpec((GMLP_GROUPS, CHUNK, CHUNK), lambda i: (0, 0, 0)),
            pl.BlockSpec((CHUNK, GMLP_GROUPS), lambda i: (0, 0)),
        ],
        out_specs=pl.BlockSpec((rows, GMLP_DIM), lambda i: (i, 0)),
        compiler_params=_params("parallel"),
        name="spatial_gating",
    )(z, z, row(ln_g), row(ln_b), w_s.astype(F32), b_s.T.astype(F32))


ROUTER_LANES = V7X_LANES


def _first_argmax(x, lane, width):
    m = jnp.max(x, axis=-1, keepdims=True)
    idx = jnp.min(jnp.where(x == m, lane, width), axis=-1, keepdims=True)
    return m, idx


def _pack_bf16_pairs(x):
    c = x.shape[1] // 2
    lo = pltpu.bitcast(x[:, :c].astype(BF16).astype(F32), jnp.int32)
    hi = pltpu.bitcast(x[:, c:].astype(BF16).astype(F32), jnp.int32)
    return hi | lax.shift_right_logical(lo, 16)


def _unpack_bf16_pairs(p):
    lo = pltpu.bitcast(lax.shift_left(p, 16), F32)
    hi = pltpu.bitcast(p & jnp.int32(-65536), F32)
    return jnp.concatenate([lo, hi], axis=1)


def _router_kernel(x_ref, g_ref, whi_ref, wlo_ref, b_ref, hn_ref, eid_ref, gate_ref, rank_ref, cnt_ref,
                   carry):
    i = pl.program_id(0)
    tm = x_ref.shape[0]

    @pl.when(i == 0)
    def _():
        carry[...] = jnp.zeros_like(carry)

    h = _rmsnorm_rows(x_ref[...], g_ref[...])
    hn_ref[...] = _pack_bf16_pairs(h)
    h_hi = h.astype(BF16)
    h_lo = (h - h_hi.astype(F32)).astype(BF16)
    logits = (jnp.dot(h_hi, whi_ref[...], preferred_element_type=F32)
              + jnp.dot(h_lo, whi_ref[...], preferred_element_type=F32)
              + jnp.dot(h_hi, wlo_ref[...], preferred_element_type=F32) + b_ref[...])
    lane = lax.broadcasted_iota(jnp.int32, logits.shape, 1)
    gl = jnp.where(lane < N_GROUPS, logits, NEG_INF)
    gmax, g_idx = _first_argmax(gl, lane, ROUTER_LANES)
    g_gate = 1.0 / jnp.sum(jnp.exp(gl - gmax), axis=-1, keepdims=True)
    e_lane = lane - N_GROUPS
    in_group = (e_lane >= g_idx * EXPERTS_PER_GROUP) & (e_lane < (g_idx + 1) * EXPERTS_PER_GROUP)
    el = jnp.where(in_group, logits, NEG_INF)
    l1, i1 = _first_argmax(el, lane, ROUTER_LANES)
    el2 = jnp.where(lane == i1, NEG_INF, el)
    l2, i2 = _first_argmax(el2, lane, ROUTER_LANES)
    r = jnp.exp(l2 - l1)
    w1 = g_gate / (1.0 + r)
    w2 = g_gate * r / (1.0 + r)
    e1 = i1 - N_GROUPS
    e2 = i2 - N_GROUPS
    hot1 = lane == e1
    hot2 = lane == e2
    hot = (hot1 | hot2).astype(BF16)
    ri = lax.broadcasted_iota(jnp.int32, (tm, tm), 0)
    ci = lax.broadcasted_iota(jnp.int32, (tm, tm), 1)
    before = (ci < ri).astype(BF16)
    seen = jnp.dot(before, hot, preferred_element_type=F32) + carry[...]
    r1 = jnp.sum(jnp.where(hot1, seen, 0.0), axis=-1, keepdims=True)
    r2 = jnp.sum(jnp.where(hot2, seen, 0.0), axis=-1, keepdims=True)
    carry[...] = carry[...] + jnp.sum(hot.astype(F32), axis=0, keepdims=True)
    eid_ref[...] = jnp.concatenate([e1, e2], axis=1)
    gate_ref[...] = jnp.concatenate([w1, w2], axis=1)
    rank_ref[...] = jnp.concatenate([r1, r2], axis=1).astype(jnp.int32)
    cnt_ref[...] = carry[...].astype(jnp.int32)


def moe_router(x, norm, w_rg, b_rg, w_re, b_re):
    t, d = x.shape
    tm = _tile(t, 256)
    pad = ROUTER_LANES - N_GROUPS - N_EXPERTS
    w = jnp.concatenate([w_rg, w_re, jnp.zeros((d, pad), F32)], axis=1)
    w_hi = w.astype(BF16)
    w_lo = (w - w_hi.astype(F32)).astype(BF16)
    b = jnp.concatenate([b_rg, b_re, jnp.zeros((pad,), F32)]).reshape(1, ROUTER_LANES)
    pair = lambda dt: jax.ShapeDtypeStruct((t, 2), dt)
    pair_spec = pl.BlockSpec((tm, 2), lambda i: (i, 0))
    w_spec = pl.BlockSpec((d, ROUTER_LANES), lambda i: (0, 0))
    return pl.pallas_call(
        _router_kernel,
        out_shape=(jax.ShapeDtypeStruct((t, d // 2), jnp.int32), pair(jnp.int32), pair(F32),
                   pair(jnp.int32), jax.ShapeDtypeStruct((1, ROUTER_LANES), jnp.int32)),
        grid=(t // tm,),
        in_specs=[
            pl.BlockSpec((tm, d), lambda i: (i, 0)),
            pl.BlockSpec((1, d), lambda i: (0, 0)),
            w_spec, w_spec,
            pl.BlockSpec((1, ROUTER_LANES), lambda i: (0, 0)),
        ],
        out_specs=(pl.BlockSpec((tm, d // 2), lambda i: (i, 0)), pair_spec, pair_spec, pair_spec,
                   pl.BlockSpec((1, ROUTER_LANES), lambda i: (0, 0))),
        scratch_shapes=[pltpu.VMEM((1, ROUTER_LANES), F32)],
        compiler_params=_params("arbitrary"),
        name="moe_router",
    )(x, norm.reshape(1, d), w_hi, w_lo, b)


def _row_copy(src_hbm, dst, src_row, dst_row, sem):
    return pltpu.make_async_copy(src_hbm.at[pl.ds(src_row, 1)], dst.at[pl.ds(dst_row, 1)], sem)


def _experts_kernel(te_ref, nxt_ref, na_ref, tok_ref, hn_hbm, wg_hbm, wu_hbm, wd_hbm, ys_ref,
                    hbuf, stage_g, stage_u, stage_d, wg_b, wu_b, wd_b, gsem, wsem):
    i = pl.program_id(0)
    tm = ys_ref.shape[0]
    n_active = na_ref[0]
    e = te_ref[i]
    run_start = (i == 0) | (e != te_ref[jnp.maximum(i - 1, 0)])

    def weight_copies(expert):
        return (pltpu.make_async_copy(wg_hbm.at[expert], stage_g, wsem.at[0]),
                pltpu.make_async_copy(wu_hbm.at[expert], stage_u, wsem.at[1]),
                pltpu.make_async_copy(wd_hbm.at[expert], stage_d, wsem.at[2]))

    def gather(tile, slot):
        def body(r, carry):
            _row_copy(hn_hbm, hbuf.at[slot], tok_ref[tile * tm + r], r, gsem.at[slot]).start()
            return carry
        lax.fori_loop(0, tm, body, 0, unroll=8)

    @pl.when((i == 0) & (n_active > 0))
    def _():
        for c in weight_copies(e):
            c.start(priority=1)
        gather(0, 0)

    @pl.when((i < n_active) & run_start)
    def _():
        for c in weight_copies(e):
            c.wait()
        wg_b[...] = stage_g[...].astype(BF16)
        wu_b[...] = stage_u[...].astype(BF16)
        wd_b[...] = stage_d[...].astype(BF16)
        nxt = nxt_ref[e]

        @pl.when(nxt >= 0)
        def _():
            for c in weight_copies(nxt):
                c.start(priority=1)

    def wait_gather(slot):
        pltpu.make_async_copy(hn_hbm.at[pl.ds(0, tm)], hbuf.at[slot], gsem.at[slot]).wait()

    @pl.when(i < n_active)
    def _():
        slot = i % 2
        wait_gather(slot)
        h = _unpack_bf16_pairs(hbuf[slot]).astype(BF16)
        for r in range(tm):
            _row_copy(hn_hbm, hbuf.at[1 - slot], tok_ref[(i + 1) * tm + r], r, gsem.at[1 - slot]).start()
        g = jnp.dot(h, wg_b[...], preferred_element_type=F32)
        u = jnp.dot(h, wu_b[...], preferred_element_type=F32)
        act = (g * _sigmoid(g) * u).astype(BF16)
        ys_ref[...] = _pack_bf16_pairs(jnp.dot(act, wd_b[...], preferred_element_type=F32))

    @pl.when((i == n_active) & (i > 0))
    def _():
        wait_gather(i % 2)

    @pl.when(i >= n_active)
    def _():
        ys_ref[...] = jnp.zeros_like(ys_ref)


def moe_experts(hn, tok_of_slot, tile_expert, next_expert, n_active, w_gate, w_up, w_down, tm):
    dp = hn.shape[1]
    d = w_gate.shape[1]
    n_slots = tok_of_slot.shape[0]
    any_spec = pl.BlockSpec(memory_space=pl.ANY)
    return pl.pallas_call(
        _experts_kernel,
        out_shape=jax.ShapeDtypeStruct((n_slots, dp), jnp.int32),
        grid_spec=pltpu.PrefetchScalarGridSpec(
            num_scalar_prefetch=4,
            grid=(n_slots // tm,),
            in_specs=[any_spec, any_spec, any_spec, any_spec],
            out_specs=pl.BlockSpec((tm, dp), lambda i, *_: (i, 0)),
            scratch_shapes=[
                pltpu.VMEM((2, tm, dp), jnp.int32),
                pltpu.VMEM((d, D_EXPERT), F32), pltpu.VMEM((d, D_EXPERT), F32),
                pltpu.VMEM((D_EXPERT, d), F32),
                pltpu.VMEM((d, D_EXPERT), BF16), pltpu.VMEM((d, D_EXPERT), BF16),
                pltpu.VMEM((D_EXPERT, d), BF16),
                pltpu.SemaphoreType.DMA((2,)), pltpu.SemaphoreType.DMA((3,)),
            ],
        ),
        compiler_params=_params("arbitrary"),
        name="moe_experts",
    )(tile_expert, next_expert, n_active, tok_of_slot, hn, w_gate, w_up, w_down)


def _combine_kernel(dest_ref, x_ref, gate_ref, ys_hbm, ng_ref, *rest, final_norm):
    if final_norm:
        o_ref, ybuf, sems = rest
    else:
        o_ref, hn_ref, ybuf, sems = rest
    i = pl.program_id(0)
    tc = x_ref.shape[0]

    last = pl.num_programs(0) - 1

    def start_row(step, slot, k):
        tok = step * tc + k
        _row_copy(ys_hbm, ybuf.at[slot, 0], dest_ref[2 * tok], k, sems.at[slot]).start()
        _row_copy(ys_hbm, ybuf.at[slot, 1], dest_ref[2 * tok + 1], k, sems.at[slot]).start()

    def wait_rows(slot):
        for c in range(2):
            pltpu.make_async_copy(ys_hbm.at[pl.ds(0, tc)], ybuf.at[slot, c], sems.at[slot]).wait()

    @pl.when(i == 0)
    def _():
        lax.fori_loop(0, tc, lambda k, c: (start_row(0, 0, k), c)[1], 0, unroll=4)

    slot = i % 2
    wait_rows(slot)
    for k in range(tc):
        start_row(jnp.minimum(i + 1, last), 1 - slot, k)
    gate = gate_ref[...]
    y = (x_ref[...] + gate[:, 0:1] * _unpack_bf16_pairs(ybuf[slot, 0])
         + gate[:, 1:2] * _unpack_bf16_pairs(ybuf[slot, 1]))
    if final_norm:
        o_ref[...] = _rmsnorm_rows(y, ng_ref[...])
    else:
        o_ref[...] = y
        hn_ref[...] = _rmsnorm_rows(y, ng_ref[...]).astype(hn_ref.dtype)

    @pl.when(i == last)
    def _():
        wait_rows(1 - slot)


def moe_combine(x, gates, dest, ys, norm_g, final_norm):
    t, d = x.shape
    tc = _tile(t, 256)
    row_spec = pl.BlockSpec((tc, d), lambda i, dst: (i, 0))
    out_shape = jax.ShapeDtypeStruct((t, d), F32)
    out_specs = row_spec
    if not final_norm:
        out_shape = (out_shape, jax.ShapeDtypeStruct((t, d), BF16))
        out_specs = (row_spec, row_spec)
    return pl.pallas_call(
        functools.partial(_combine_kernel, final_norm=final_norm),
        out_shape=out_shape,
        grid_spec=pltpu.PrefetchScalarGridSpec(
            num_scalar_prefetch=1,
            grid=(t // tc,),
            in_specs=[
                row_spec,
                pl.BlockSpec((tc, 2), lambda i, dst: (i, 0)),
                pl.BlockSpec(memory_space=pl.ANY),
                pl.BlockSpec((1, d), lambda i, dst: (0, 0)),
            ],
            out_specs=out_specs,
            scratch_shapes=[pltpu.VMEM((2, 2, tc, ys.shape[1]), jnp.int32),
                            pltpu.SemaphoreType.DMA((2,))],
        ),
        compiler_params=_params("arbitrary"),
        name="moe_combine",
    )(dest, x, gates, ys, norm_g.reshape(1, d))


def hierarchical_moe(x, norm, w_rg, b_rg, w_re, b_re, w_gate, w_up, w_down, out_norm_g, final_norm):
    t, d = x.shape
    tm = _tile(t, 256)
    n_slots = 2 * t + N_EXPERTS * tm
    hn, eid, gates, rank, counts = moe_router(x, norm, w_rg, b_rg, w_re, b_re)
    counts = counts[0, :N_EXPERTS]
    padded = (counts + tm - 1) // tm * tm
    ends = jnp.cumsum(padded)
    dest = ((ends - padded)[eid] + rank).reshape(2 * t).astype(jnp.int32)
    n_active = (ends[-1] // tm).astype(jnp.int32)
    tile_row = jnp.minimum(jnp.arange(n_slots // tm, dtype=jnp.int32), n_active - 1) * tm
    tile_expert = jnp.sum(tile_row[:, None] >= ends[None, :], axis=1).astype(jnp.int32)
    tok_of_slot = jnp.zeros((n_slots,), jnp.int32).at[dest].set(
        jnp.arange(2 * t, dtype=jnp.int32) // 2, unique_indices=True)
    ids = jnp.arange(N_EXPERTS, dtype=jnp.int32)
    owners = jnp.where(padded > 0, ids, N_EXPERTS)
    later = jnp.min(jnp.where(ids[None, :] > ids[:, None], owners[None, :], N_EXPERTS), axis=1)
    next_expert = jnp.where(later < N_EXPERTS, later, -1).astype(jnp.int32)
    ys = moe_experts(hn, tok_of_slot, tile_expert, next_expert, n_active.reshape(1),
                     w_gate, w_up, w_down, tm)
    return moe_combine(x, gates, dest, ys, out_norm_g, final_norm)


def kernel(x, positions, l0_norm_mix, l0_w_in, l0_sinks, l0_conv_w, l0_conv_b, l0_ln_g, l0_ln_b, l0_w_out, l0_norm_ffn, l0_w_rg, l0_b_rg, l0_w_re, l0_b_re, l0_w_gate, l0_w_up, l0_w_down, l1_norm_mix, l1_w_in, l1_ln_g, l1_ln_b, l1_w_s, l1_b_s, l1_w_out, l1_norm_ffn, l1_w_rg, l1_b_rg, l1_w_re, l1_b_re, l1_w_gate, l1_w_up, l1_w_down, final_norm):
    b, s, d = x.shape
    t = b * s
    x = x.reshape(t, d)
    bf = lambda w: w.astype(BF16)

    proj = matmul([prenorm(x, l0_norm_mix)], [bf(l0_w_in)], out_dtype=BF16, name="l0_in_proj")
    attn = attention(proj, positions, l0_sinks, s)
    conv = conformer_conv(proj, l0_conv_w, l0_conv_b, l0_ln_g, l0_ln_b, s)
    w_out0 = bf(l0_w_out)
    x = matmul([attn, conv], [w_out0[:A_Q], w_out0[A_Q:]], res=x, out_dtype=F32, name="l0_out_proj")
    x, hn = hierarchical_moe(x, l0_norm_ffn, l0_w_rg, l0_b_rg, l0_w_re, l0_b_re,
                             l0_w_gate, l0_w_up, l0_w_down, l1_norm_mix, final_norm=False)
    z = matmul([hn], [bf(l1_w_in)], act="gelu", out_dtype=BF16, name="l1_in_proj")
    gated = spatial_gating(z, l1_ln_g, l1_ln_b, l1_w_s, l1_b_s)
    x = matmul([gated], [bf(l1_w_out)], res=x, out_dtype=F32, name="l1_out_proj")
    x = hierarchical_moe(x, l1_norm_ffn, l1_w_rg, l1_b_rg, l1_w_re, l1_b_re,
                         l1_w_gate, l1_w_up, l1_w_down, final_norm, final_norm=True)
    return x.reshape(b, s, d)
```

```python
import functools

import numpy as np
import jax
import jax.numpy as jnp
from jax import lax
from jax.experimental import pallas as pl
from jax.experimental.pallas import tpu as pltpu

D_MODEL = 4096
HEAD_DIM = 64
N_Q_HEADS = 32
N_KV_HEADS = 8
Q_PER_KV = N_Q_HEADS // N_KV_HEADS
BLOCK = 128
ROPE_THETA = 500000.0
ROPE_DIM = HEAD_DIM // 4
ROPE_HALF = ROPE_DIM // 2
A_Q = N_Q_HEADS * HEAD_DIM
A_KV = N_KV_HEADS * HEAD_DIM
CONV_CH = D_MODEL // 2
CONV_WIDTH = 31
CONV_HALO = 32
IN_EVEN = A_Q + 2 * A_KV + 2 * CONV_CH
CHUNK = 128
GMLP_DIM = D_MODEL
GMLP_GROUPS = 16
GMLP_GROUP_DIM = GMLP_DIM // GMLP_GROUPS
N_GROUPS = 4
EXPERTS_PER_GROUP = 8
N_EXPERTS = N_GROUPS * EXPERTS_PER_GROUP
D_EXPERT = 512
EPS = 1e-5
NEG_INF = -1e30
LOG2_E = float(np.log2(np.e))

V7X_LANES = 128
V7X_SUBLANES = 8
V7X_VMEM_BYTES = 64 * 1024 * 1024
VMEM_LIMIT = 56 * 1024 * 1024

F32 = jnp.float32
BF16 = jnp.bfloat16


def _tile(n, want):
    t = min(n, want)
    while n % t:
        t //= 2
    return t


def _params(*sem):
    return pltpu.CompilerParams(dimension_semantics=sem, vmem_limit_bytes=VMEM_LIMIT)


def _sigmoid(x):
    return 1.0 / (1.0 + jnp.exp(-x))


def _gelu_tanh(x):
    c = np.float32(np.sqrt(2.0 / np.pi))
    return x * (0.5 * (1.0 + jnp.tanh(c * (x + 0.044715 * (x * x * x)))))


def _rmsnorm_rows(x, g):
    ms = jnp.mean(x * x, axis=-1, keepdims=True)
    return x * lax.rsqrt(ms + EPS) * g


def _prenorm_kernel(x_ref, g_ref, o_ref):
    o_ref[...] = _rmsnorm_rows(x_ref[...], g_ref[...]).astype(o_ref.dtype)


def prenorm(x, g):
    t, d = x.shape
    tm = _tile(t, 512)
    return pl.pallas_call(
        _prenorm_kernel,
        out_shape=jax.ShapeDtypeStruct((t, d), BF16),
        grid=(t // tm,),
        in_specs=[pl.BlockSpec((tm, d), lambda i: (i, 0)), pl.BlockSpec((1, d), lambda i: (0, 0))],
        out_specs=pl.BlockSpec((tm, d), lambda i: (i, 0)),
        compiler_params=_params("parallel"),
        name="prenorm",
    )(x, g.reshape(1, d))


def _matmul_kernel(*refs, n_a, has_res, act):
    a_refs, w_refs = refs[:n_a], refs[n_a:2 * n_a]
    o_ref = refs[-1]
    acc = None
    for a_ref, w_ref in zip(a_refs, w_refs):
        part = jnp.dot(a_ref[...], w_ref[...], preferred_element_type=F32)
        acc = part if acc is None else acc + part
    if has_res:
        acc = acc + refs[2 * n_a][...]
    if act == "gelu":
        acc = _gelu_tanh(acc)
    o_ref[...] = acc.astype(o_ref.dtype)


def matmul(a_list, w_list, *, res=None, act=None, out_dtype, name):
    t = a_list[0].shape[0]
    n = w_list[0].shape[1]
    tm = _tile(t, 1024)
    tn = _tile(n, 1024)
    n_a = len(a_list)
    in_specs = [pl.BlockSpec((tm, a.shape[1]), lambda i, j: (i, 0)) for a in a_list]
    in_specs += [pl.BlockSpec((w.shape[0], tn), lambda i, j: (0, j)) for w in w_list]
    args = [*a_list, *w_list]
    if res is not None:
        in_specs.append(pl.BlockSpec((tm, tn), lambda i, j: (i, j)))
        args.append(res)
    return pl.pallas_call(
        functools.partial(_matmul_kernel, n_a=n_a, has_res=res is not None, act=act),
        out_shape=jax.ShapeDtypeStruct((t, n), out_dtype),
        grid=(t // tm, n // tn),
        in_specs=in_specs,
        out_specs=pl.BlockSpec((tm, tn), lambda i, j: (i, j)),
        compiler_params=_params("parallel", "arbitrary"),
        name=name,
    )(*args)


def _rope_tables(pos_ref, invf_ref):
    ang = pos_ref[...].astype(F32) * invf_ref[...]
    c = jnp.cos(ang)
    s = jnp.sin(ang)
    lane = lax.broadcasted_iota(jnp.int32, ang.shape, 1) % HEAD_DIM
    s_lo = jnp.where(lane < ROPE_HALF, -s, 0.0)
    s_hi = jnp.where((lane >= ROPE_HALF) & (lane < ROPE_DIM), s, 0.0)
    return c, s_lo, s_hi


def _apply_rope(x, tables):
    c, s_lo, s_hi = tables
    w = x.shape[1]
    reps = w // c.shape[1]
    c, s_lo, s_hi = (jnp.concatenate([t] * reps, axis=1) for t in (c, s_lo, s_hi))
    return (x * c + pltpu.roll(x, w - ROPE_HALF, 1) * s_lo + pltpu.roll(x, ROPE_HALF, 1) * s_hi)


def _attn_kernel(sink_ref, pos_c_ref, pos_p_ref, invf_ref, q_ref, kc_ref, kp_ref, vc_ref, vp_ref,
                 o_ref, *, blocks_per_seq):
    i = pl.program_id(0)
    not_first = (i % blocks_per_seq) > 0
    tab_c = _rope_tables(pos_c_ref, invf_ref)
    tab_p = _rope_tables(pos_p_ref, invf_ref)
    scale = HEAD_DIM ** -0.5 * LOG2_E
    q = _apply_rope(q_ref[...].astype(F32), tab_c) * scale
    k = jnp.concatenate([_apply_rope(kp_ref[...].astype(F32), tab_p),
                         _apply_rope(kc_ref[...].astype(F32), tab_c)], axis=0)
    v = jnp.concatenate([vp_ref[...], vc_ref[...]], axis=0).astype(F32)

    qi = lax.broadcasted_iota(jnp.int32, (Q_PER_KV * BLOCK, 2 * BLOCK), 0) % BLOCK + BLOCK
    kj = lax.broadcasted_iota(jnp.int32, (Q_PER_KV * BLOCK, 2 * BLOCK), 1)
    valid = (kj <= qi) & (qi - kj < BLOCK) & (not_first | (kj >= BLOCK))

    col = V7X_LANES
    low_q = lax.broadcasted_iota(jnp.int32, (BLOCK, col), 1) < HEAD_DIM
    low_kv = lax.broadcasted_iota(jnp.int32, (2 * BLOCK, col), 1) < HEAD_DIM
    ones = jnp.ones((2 * BLOCK, col), BF16)
    outs = []
    for c in range(A_KV // col):
        kcol = k[:, c * col:(c + 1) * col]
        vcol = v[:, c * col:(c + 1) * col]
        kswap = pltpu.roll(kcol, HEAD_DIM, 1)
        vswap = pltpu.roll(vcol, HEAD_DIM, 1)
        for half in range(col // HEAD_DIM):
            g = c * (col // HEAD_DIM) + half
            if half == 0:
                kk, vv = jnp.where(low_kv, kcol, kswap), jnp.where(low_kv, vcol, vswap)
            else:
                kk, vv = jnp.where(low_kv, kswap, kcol), jnp.where(low_kv, vswap, vcol)
            vv_ones = jnp.concatenate([vv.astype(BF16), ones], axis=1)
            qa = q[:, (2 * g) * col:(2 * g + 1) * col]
            qb = q[:, (2 * g + 1) * col:(2 * g + 2) * col]
            qs = jnp.concatenate([jnp.where(low_q, qa, 0.0), jnp.where(low_q, 0.0, qa),
                                  jnp.where(low_q, qb, 0.0), jnp.where(low_q, 0.0, qb)],
                                 axis=0).astype(BF16)
            s = lax.dot_general(qs, kk.astype(BF16), (((1,), (1,)), ((), ())),
                                preferred_element_type=F32)
            s = jnp.where(valid, s, NEG_INF)
            sink = jnp.concatenate(
                [jnp.full((BLOCK, 1), sink_ref[g * Q_PER_KV + h] * LOG2_E, F32)
                 for h in range(Q_PER_KV)], axis=0)
            m = jnp.maximum(jnp.max(s, axis=-1, keepdims=True), sink)
            p = jnp.exp2(s - m).astype(BF16)
            o = jnp.dot(p, vv_ones, preferred_element_type=F32)
            o = o[:, :col] / (o[:, col:] + jnp.exp2(sink - m))
            outs.append(jnp.where(low_q, o[0:BLOCK], o[BLOCK:2 * BLOCK]))
            outs.append(jnp.where(low_q, o[2 * BLOCK:3 * BLOCK], o[3 * BLOCK:4 * BLOCK]))
    o_ref[...] = jnp.concatenate(outs, axis=1).astype(o_ref.dtype)


def attention(proj, positions, sinks, seq_len):
    t = proj.shape[0]
    nb = t // BLOCK
    invf = ROPE_THETA ** (-np.arange(0, ROPE_DIM, 2, dtype=np.float32) / ROPE_DIM)
    lane = np.arange(V7X_LANES) % HEAD_DIM
    invf_row = np.where(lane < ROPE_DIM, invf[lane % ROPE_HALF], 0.0).astype(np.float32)
    pos = positions.reshape(t, 1)
    kcol = A_Q // A_KV
    prev = lambda i, s: (jnp.maximum(i - 1, 0), 0)
    return pl.pallas_call(
        functools.partial(_attn_kernel, blocks_per_seq=seq_len // BLOCK),
        out_shape=jax.ShapeDtypeStruct((t, A_Q), BF16),
        grid_spec=pltpu.PrefetchScalarGridSpec(
            num_scalar_prefetch=1,
            grid=(nb,),
            in_specs=[
                pl.BlockSpec((BLOCK, 1), lambda i, s: (i, 0)),
                pl.BlockSpec((BLOCK, 1), prev),
                pl.BlockSpec((1, V7X_LANES), lambda i, s: (0, 0)),
                pl.BlockSpec((BLOCK, A_Q), lambda i, s: (i, 0)),
                pl.BlockSpec((BLOCK, A_KV), lambda i, s: (i, kcol)),
                pl.BlockSpec((BLOCK, A_KV), lambda i, s: (jnp.maximum(i - 1, 0), kcol)),
                pl.BlockSpec((BLOCK, A_KV), lambda i, s: (i, kcol + 1)),
                pl.BlockSpec((BLOCK, A_KV), lambda i, s: (jnp.maximum(i - 1, 0), kcol + 1)),
            ],
            out_specs=pl.BlockSpec((BLOCK, A_Q), lambda i, s: (i, 0)),
        ),
        compiler_params=_params("parallel"),
        name="swa_attention",
    )(sinks.astype(F32), pos, pos, jnp.asarray(invf_row).reshape(1, V7X_LANES),
      proj, proj, proj, proj, proj)


def _conv_kernel(a0_ref, a1_ref, g0_ref, g1_ref, w_ref, cb_ref, lg_ref, lb_ref, o_ref,
                 hbuf, ybuf, hsh, *, steps_per_seq):
    i = pl.program_id(0)
    ts = o_ref.shape[0]
    half = CONV_CH // 2

    @pl.when(i % steps_per_seq == 0)
    def _():
        hbuf[0:CONV_HALO, :] = jnp.zeros((CONV_HALO, CONV_CH), F32)

    @pl.when(i % steps_per_seq != 0)
    def _():
        hbuf[0:CONV_HALO, :] = hbuf[ts:ts + CONV_HALO, :]

    for c, (a_ref, g_ref) in enumerate(((a0_ref, g0_ref), (a1_ref, g1_ref))):
        hbuf[CONV_HALO:CONV_HALO + ts, c * half:(c + 1) * half] = (
            a_ref[...].astype(F32) * _sigmoid(g_ref[...].astype(F32)))

    sub = V7X_SUBLANES
    first_tap = CONV_HALO - (CONV_WIDTH - 1)
    h_all = hbuf[...]
    for s in range(1, sub):
        hsh[s - 1] = pltpu.roll(h_all, ts + CONV_HALO - s, 0)

    rows = 2 * sub
    lanes = CONV_CH // 4

    def body(r, carry):
        base = pl.multiple_of(r * rows, rows)
        for c in range(CONV_CH // lanes):
            csl = slice(c * lanes, (c + 1) * lanes)
            acc = [jnp.broadcast_to(cb_ref[:, csl], (sub, lanes)) for _ in range(rows // sub)]
            for j in range(CONV_WIDTH):
                off = first_tap + j
                w = w_ref[j * sub:(j + 1) * sub, csl]
                for b in range(rows // sub):
                    start = base + (off // sub + b) * sub
                    if off % sub == 0:
                        tap = hbuf[pl.ds(start, sub), csl]
                    else:
                        tap = hsh[off % sub - 1, pl.ds(start, sub), csl]
                    acc[b] = acc[b] + tap * w
            for b in range(rows // sub):
                ybuf[pl.ds(base + b * sub, sub), csl] = acc[b]
        return carry

    lax.fori_loop(0, ts // rows, body, 0)

    y = ybuf[...]
    mu = jnp.mean(y, axis=-1, keepdims=True)
    yc = y - mu
    var = jnp.mean(yc * yc, axis=-1, keepdims=True)
    z = yc * lax.rsqrt(var + EPS) * lg_ref[...] + lb_ref[...]
    o_ref[...] = (z * _sigmoid(z)).astype(o_ref.dtype)


def conformer_conv(proj, conv_w, conv_b, ln_g, ln_b, seq_len):
    t = proj.shape[0]
    ts = _tile(seq_len, 256)
    half = CONV_CH // 2
    col0 = (A_Q + 2 * A_KV) // half
    row = lambda v: v.reshape(1, CONV_CH).astype(F32)
    full = lambda shape: pl.BlockSpec(shape, lambda i: (0, 0))
    return pl.pallas_call(
        functools.partial(_conv_kernel, steps_per_seq=seq_len // ts),
        out_shape=jax.ShapeDtypeStruct((t, CONV_CH), BF16),
        grid=(t // ts,),
        in_specs=[
            pl.BlockSpec((ts, half), lambda i: (i, col0)),
            pl.BlockSpec((ts, half), lambda i: (i, col0 + 1)),
            pl.BlockSpec((ts, half), lambda i: (i, col0 + 2)),
            pl.BlockSpec((ts, half), lambda i: (i, col0 + 3)),
            full((CONV_WIDTH * V7X_SUBLANES, CONV_CH)),
            full((1, CONV_CH)), full((1, CONV_CH)), full((1, CONV_CH)),
        ],
        out_specs=pl.BlockSpec((ts, CONV_CH), lambda i: (i, 0)),
        scratch_shapes=[pltpu.VMEM((ts + CONV_HALO, CONV_CH), F32), pltpu.VMEM((ts, CONV_CH), F32),
                        pltpu.VMEM((V7X_SUBLANES - 1, ts + CONV_HALO, CONV_CH), F32)],
        compiler_params=_params("arbitrary"),
        name="conformer_conv",
    )(proj, proj, proj, proj, jnp.repeat(conv_w.astype(F32), V7X_SUBLANES, axis=0),
      row(conv_b), row(ln_g), row(ln_b))


def _sgu_kernel(u_ref, v_ref, lg_ref, lb_ref, ws_ref, bs_ref, o_ref):
    rows = o_ref.shape[0]
    v = v_ref[...].astype(F32)
    mu = jnp.mean(v, axis=-1, keepdims=True)
    vc = v - mu
    var = jnp.mean(vc * vc, axis=-1, keepdims=True)
    vn = (vc * lax.rsqrt(var + EPS) * lg_ref[...] + lb_ref[...]).astype(BF16)
    ti = lax.broadcasted_iota(jnp.int32, (CHUNK, CHUNK), 0)
    si = lax.broadcasted_iota(jnp.int32, (CHUNK, CHUNK), 1)
    causal = si <= ti
    for c in range(rows // CHUNK):
        rsl = slice(c * CHUNK, (c + 1) * CHUNK)
        for g in range(GMLP_GROUPS):
            csl = slice(g * GMLP_GROUP_DIM, (g + 1) * GMLP_GROUP_DIM)
            ws = jnp.where(causal, ws_ref[g], 0.0).astype(BF16)
            sv = jnp.dot(ws, vn[rsl, csl], preferred_element_type=F32) + bs_ref[:, g:g + 1]
            o_ref[rsl, csl] = (u_ref[rsl, csl].astype(F32) * sv).astype(o_ref.dtype)


def spatial_gating(z, ln_g, ln_b, w_s, b_s):
    t = z.shape[0]
    rows = _tile(t, 2 * CHUNK)
    row = lambda v: v.reshape(1, GMLP_DIM).astype(F32)
    return pl.pallas_call(
        _sgu_kernel,
        out_shape=jax.ShapeDtypeStruct((t, GMLP_DIM), BF16),
        grid=(t // rows,),
        in_specs=[
            pl.BlockSpec((rows, GMLP_DIM), lambda i: (i, 0)),
            pl.BlockSpec((rows, GMLP_DIM), lambda i: (i, 1)),
            pl.BlockSpec((1, GMLP_DIM), lambda i: (0, 0)),
            pl.BlockSpec((1, GMLP_DIM), lambda i: (0, 0)),
            pl.BlockSpec((GMLP_GROUPS, CHUNK, CHUNK), lambda i: (0, 0, 0)),
            pl.BlockSpec((CHUNK, GMLP_GROUPS), lambda i: (0, 0)),
        ],
        out_specs=pl.BlockSpec((rows, GMLP_DIM), lambda i: (i, 0)),
        compiler_params=_params("parallel"),
        name="spatial_gating",
    )(z, z, row(ln_g), row(ln_b), w_s.astype(F32), b_s.T.astype(F32))


ROUTER_LANES = V7X_LANES


def _first_argmax(x, lane, width):
    m = jnp.max(x, axis=-1, keepdims=True)
    idx = jnp.min(jnp.where(x == m, lane, width), axis=-1, keepdims=True)
    return m, idx


def _pack_bf16_pairs(x):
    c = x.shape[1] // 2
    lo = pltpu.bitcast(x[:, :c].astype(BF16).astype(F32), jnp.int32)
    hi = pltpu.bitcast(x[:, c:].astype(BF16).astype(F32), jnp.int32)
    return hi | lax.shift_right_logical(lo, 16)


def _unpack_bf16_pairs(p):
    lo = pltpu.bitcast(lax.shift_left(p, 16), F32)
    hi = pltpu.bitcast(p & jnp.int32(-65536), F32)
    return jnp.concatenate([lo, hi], axis=1)


ROW_SLAB = D_MODEL // 2 // V7X_LANES


def _store_row_slabs(ref, packed):
    rows = packed.shape[0]
    for s in range(ROW_SLAB):
        ref[pl.ds(s, rows, stride=ROW_SLAB), :] = packed[:, s * V7X_LANES:(s + 1) * V7X_LANES]


def _load_row_slabs(ref, rows):
    return jnp.concatenate([ref[pl.ds(s, rows, stride=ROW_SLAB), :] for s in range(ROW_SLAB)], axis=1)


def _router_kernel(x_ref, g_ref, whi_ref, wlo_ref, b_ref, hn_ref, eid_ref, gate_ref, rank_ref, cnt_ref,
                   carry):
    i = pl.program_id(0)
    tm = x_ref.shape[0]

    @pl.when(i == 0)
    def _():
        carry[...] = jnp.zeros_like(carry)

    h = _rmsnorm_rows(x_ref[...], g_ref[...])
    _store_row_slabs(hn_ref, _pack_bf16_pairs(h))
    h_hi = h.astype(BF16)
    h_lo = (h - h_hi.astype(F32)).astype(BF16)
    logits = (jnp.dot(h_hi, whi_ref[...], preferred_element_type=F32)
              + jnp.dot(h_lo, whi_ref[...], preferred_element_type=F32)
              + jnp.dot(h_hi, wlo_ref[...], preferred_element_type=F32) + b_ref[...])
    lane = lax.broadcasted_iota(jnp.int32, logits.shape, 1)
    gl = jnp.where(lane < N_GROUPS, logits, NEG_INF)
    gmax, g_idx = _first_argmax(gl, lane, ROUTER_LANES)
    g_gate = 1.0 / jnp.sum(jnp.exp(gl - gmax), axis=-1, keepdims=True)
    e_lane = lane - N_GROUPS
    in_group = (e_lane >= g_idx * EXPERTS_PER_GROUP) & (e_lane < (g_idx + 1) * EXPERTS_PER_GROUP)
    el = jnp.where(in_group, logits, NEG_INF)
    l1, i1 = _first_argmax(el, lane, ROUTER_LANES)
    el2 = jnp.where(lane == i1, NEG_INF, el)
    l2, i2 = _first_argmax(el2, lane, ROUTER_LANES)
    r = jnp.exp(l2 - l1)
    w1 = g_gate / (1.0 + r)
    w2 = g_gate * r / (1.0 + r)
    e1 = i1 - N_GROUPS
    e2 = i2 - N_GROUPS
    hot1 = lane == e1
    hot2 = lane == e2
    hot = (hot1 | hot2).astype(BF16)
    ri = lax.broadcasted_iota(jnp.int32, (tm, tm), 0)
    ci = lax.broadcasted_iota(jnp.int32, (tm, tm), 1)
    before = (ci < ri).astype(BF16)
    seen = jnp.dot(before, hot, preferred_element_type=F32) + carry[...]
    r1 = jnp.sum(jnp.where(hot1, seen, 0.0), axis=-1, keepdims=True)
    r2 = jnp.sum(jnp.where(hot2, seen, 0.0), axis=-1, keepdims=True)
    carry[...] = carry[...] + jnp.sum(hot.astype(F32), axis=0, keepdims=True)
    eid_ref[...] = jnp.concatenate([e1, e2], axis=1)
    gate_ref[...] = jnp.concatenate([w1, w2], axis=1)
    rank_ref[...] = jnp.concatenate([r1, r2], axis=1).astype(jnp.int32)
    cnt_ref[...] = carry[...].astype(jnp.int32)


def moe_router(x, norm, w_rg, b_rg, w_re, b_re):
    t, d = x.shape
    tm = _tile(t, 256)
    pad = ROUTER_LANES - N_GROUPS - N_EXPERTS
    w = jnp.concatenate([w_rg, w_re, jnp.zeros((d, pad), F32)], axis=1)
    w_hi = w.astype(BF16)
    w_lo = (w - w_hi.astype(F32)).astype(BF16)
    b = jnp.concatenate([b_rg, b_re, jnp.zeros((pad,), F32)]).reshape(1, ROUTER_LANES)
    pair = lambda dt: jax.ShapeDtypeStruct((t, 2), dt)
    pair_spec = pl.BlockSpec((tm, 2), lambda i: (i, 0))
    w_spec = pl.BlockSpec((d, ROUTER_LANES), lambda i: (0, 0))
    return pl.pallas_call(
        _router_kernel,
        out_shape=(jax.ShapeDtypeStruct((t * ROW_SLAB, V7X_LANES), jnp.int32), pair(jnp.int32), pair(F32),
                   pair(jnp.int32), jax.ShapeDtypeStruct((1, ROUTER_LANES), jnp.int32)),
        grid=(t // tm,),
        in_specs=[
            pl.BlockSpec((tm, d), lambda i: (i, 0)),
            pl.BlockSpec((1, d), lambda i: (0, 0)),
            w_spec, w_spec,
            pl.BlockSpec((1, ROUTER_LANES), lambda i: (0, 0)),
        ],
        out_specs=(pl.BlockSpec((tm * ROW_SLAB, V7X_LANES), lambda i: (i, 0)), pair_spec, pair_spec,
                   pair_spec,
                   pl.BlockSpec((1, ROUTER_LANES), lambda i: (0, 0))),
        scratch_shapes=[pltpu.VMEM((1, ROUTER_LANES), F32)],
        compiler_params=_params("arbitrary"),
        name="moe_router",
    )(x, norm.reshape(1, d), w_hi, w_lo, b)


def _row_copy(src_hbm, dst, src_row, dst_row, sem):
    return pltpu.make_async_copy(src_hbm.at[pl.ds(src_row * ROW_SLAB, ROW_SLAB)],
                                 dst.at[pl.ds(dst_row * ROW_SLAB, ROW_SLAB)], sem)


def _experts_kernel(te_ref, nxt_ref, na_ref, tok_ref, hn_hbm, wg_hbm, wu_hbm, wd_hbm, ys_ref,
                    hbuf, stage_g, stage_u, stage_d, wg_b, wu_b, wd_b, gsem, wsem):
    i = pl.program_id(0)
    tm = ys_ref.shape[0] // ROW_SLAB
    n_active = na_ref[0]
    e = te_ref[i]
    run_start = (i == 0) | (e != te_ref[jnp.maximum(i - 1, 0)])

    def weight_copies(expert):
        return (pltpu.make_async_copy(wg_hbm.at[expert], stage_g, wsem.at[0]),
                pltpu.make_async_copy(wu_hbm.at[expert], stage_u, wsem.at[1]),
                pltpu.make_async_copy(wd_hbm.at[expert], stage_d, wsem.at[2]))

    def gather(tile, slot):
        def body(r, carry):
            _row_copy(hn_hbm, hbuf.at[slot], tok_ref[tile * tm + r], r, gsem.at[slot]).start()
            return carry
        lax.fori_loop(0, tm, body, 0, unroll=8)

    @pl.when((i == 0) & (n_active > 0))
    def _():
        for c in weight_copies(e):
            c.start(priority=1)
        gather(0, 0)

    @pl.when((i < n_active) & run_start)
    def _():
        for c in weight_copies(e):
            c.wait()
        wg_b[...] = stage_g[...].astype(BF16)
        wu_b[...] = stage_u[...].astype(BF16)
        wd_b[...] = stage_d[...].astype(BF16)
        nxt = nxt_ref[e]

        @pl.when(nxt >= 0)
        def _():
            for c in weight_copies(nxt):
                c.start(priority=1)

    def wait_gather(slot):
        pltpu.make_async_copy(hn_hbm.at[pl.ds(0, tm * ROW_SLAB)], hbuf.at[slot], gsem.at[slot]).wait()

    @pl.when(i < n_active)
    def _():
        slot = i % 2
        wait_gather(slot)
        h = _unpack_bf16_pairs(_load_row_slabs(hbuf.at[slot], tm)).astype(BF16)
        for r in range(tm):
            _row_copy(hn_hbm, hbuf.at[1 - slot], tok_ref[(i + 1) * tm + r], r, gsem.at[1 - slot]).start()
        g = jnp.dot(h, wg_b[...], preferred_element_type=F32)
        u = jnp.dot(h, wu_b[...], preferred_element_type=F32)
        act = (g * _sigmoid(g) * u).astype(BF16)
        _store_row_slabs(ys_ref, _pack_bf16_pairs(jnp.dot(act, wd_b[...], preferred_element_type=F32)))

    @pl.when((i == n_active) & (i > 0))
    def _():
        wait_gather(i % 2)

    @pl.when(i >= n_active)
    def _():
        ys_ref[...] = jnp.zeros_like(ys_ref)


def moe_experts(hn, tok_of_slot, tile_expert, next_expert, n_active, w_gate, w_up, w_down, tm):
    d = w_gate.shape[1]
    n_slots = tok_of_slot.shape[0]
    any_spec = pl.BlockSpec(memory_space=pl.ANY)
    return pl.pallas_call(
        _experts_kernel,
        out_shape=jax.ShapeDtypeStruct((n_slots * ROW_SLAB, V7X_LANES), jnp.int32),
        grid_spec=pltpu.PrefetchScalarGridSpec(
            num_scalar_prefetch=4,
            grid=(n_slots // tm,),
            in_specs=[any_spec, any_spec, any_spec, any_spec],
            out_specs=pl.BlockSpec((tm * ROW_SLAB, V7X_LANES), lambda i, *_: (i, 0)),
            scratch_shapes=[
                pltpu.VMEM((2, tm * ROW_SLAB, V7X_LANES), jnp.int32),
                pltpu.VMEM((d, D_EXPERT), F32), pltpu.VMEM((d, D_EXPERT), F32),
                pltpu.VMEM((D_EXPERT, d), F32),
                pltpu.VMEM((d, D_EXPERT), BF16), pltpu.VMEM((d, D_EXPERT), BF16),
                pltpu.VMEM((D_EXPERT, d), BF16),
                pltpu.SemaphoreType.DMA((2,)), pltpu.SemaphoreType.DMA((3,)),
            ],
        ),
        compiler_params=_params("arbitrary"),
        name="moe_experts",
    )(tile_expert, next_expert, n_active, tok_of_slot, hn, w_gate, w_up, w_down)


def _combine_kernel(dest_ref, x_ref, gate_ref, ys_hbm, ng_ref, *rest, final_norm):
    if final_norm:
        o_ref, ybuf, sems = rest
    else:
        o_ref, hn_ref, ybuf, sems = rest
    i = pl.program_id(0)
    tc = x_ref.shape[0]

    last = pl.num_programs(0) - 1

    def start_row(step, slot, k):
        tok = step * tc + k
        _row_copy(ys_hbm, ybuf.at[slot, 0], dest_ref[2 * tok], k, sems.at[slot]).start()
        _row_copy(ys_hbm, ybuf.at[slot, 1], dest_ref[2 * tok + 1], k, sems.at[slot]).start()

    def wait_rows(slot):
        for c in range(2):
            pltpu.make_async_copy(ys_hbm.at[pl.ds(0, tc * ROW_SLAB)], ybuf.at[slot, c],
                                  sems.at[slot]).wait()

    @pl.when(i == 0)
    def _():
        lax.fori_loop(0, tc, lambda k, c: (start_row(0, 0, k), c)[1], 0, unroll=4)

    slot = i % 2
    wait_rows(slot)
    for k in range(tc):
        start_row(jnp.minimum(i + 1, last), 1 - slot, k)
    gate = gate_ref[...]
    y = (x_ref[...] + gate[:, 0:1] * _unpack_bf16_pairs(_load_row_slabs(ybuf.at[slot, 0], tc))
         + gate[:, 1:2] * _unpack_bf16_pairs(_load_row_slabs(ybuf.at[slot, 1], tc)))
    if final_norm:
        o_ref[...] = _rmsnorm_rows(y, ng_ref[...])
    else:
        o_ref[...] = y
        hn_ref[...] = _rmsnorm_rows(y, ng_ref[...]).astype(hn_ref.dtype)

    @pl.when(i == last)
    def _():
        wait_rows(1 - slot)


def moe_combine(x, gates, dest, ys, norm_g, final_norm):
    t, d = x.shape
    tc = _tile(t, 256)
    row_spec = pl.BlockSpec((tc, d), lambda i, dst: (i, 0))
    out_shape = jax.ShapeDtypeStruct((t, d), F32)
    out_specs = row_spec
    if not final_norm:
        out_shape = (out_shape, jax.ShapeDtypeStruct((t, d), BF16))
        out_specs = (row_spec, row_spec)
    return pl.pallas_call(
        functools.partial(_combine_kernel, final_norm=final_norm),
        out_shape=out_shape,
        grid_spec=pltpu.PrefetchScalarGridSpec(
            num_scalar_prefetch=1,
            grid=(t // tc,),
            in_specs=[
                row_spec,
                pl.BlockSpec((tc, 2), lambda i, dst: (i, 0)),
                pl.BlockSpec(memory_space=pl.ANY),
                pl.BlockSpec((1, d), lambda i, dst: (0, 0)),
            ],
            out_specs=out_specs,
            scratch_shapes=[pltpu.VMEM((2, 2, tc * ROW_SLAB, V7X_LANES), jnp.int32),
                            pltpu.SemaphoreType.DMA((2,))],
        ),
        compiler_params=_params("arbitrary"),
        name="moe_combine",
    )(dest, x, gates, ys, norm_g.reshape(1, d))


def hierarchical_moe(x, norm, w_rg, b_rg, w_re, b_re, w_gate, w_up, w_down, out_norm_g, final_norm):
    t, d = x.shape
    tm = _tile(t, 256)
    n_slots = 2 * t + N_EXPERTS * tm
    hn, eid, gates, rank, counts = moe_router(x, norm, w_rg, b_rg, w_re, b_re)
    counts = counts[0, :N_EXPERTS]
    padded = (counts + tm - 1) // tm * tm
    ends = jnp.cumsum(padded)
    dest = ((ends - padded)[eid] + rank).reshape(2 * t).astype(jnp.int32)
    n_active = (ends[-1] // tm).astype(jnp.int32)
    tile_row = jnp.minimum(jnp.arange(n_slots // tm, dtype=jnp.int32), n_active - 1) * tm
    tile_expert = jnp.sum(tile_row[:, None] >= ends[None, :], axis=1).astype(jnp.int32)
    tok_of_slot = jnp.zeros((n_slots,), jnp.int32).at[dest].set(
        jnp.arange(2 * t, dtype=jnp.int32) // 2, unique_indices=True)
    ids = jnp.arange(N_EXPERTS, dtype=jnp.int32)
    owners = jnp.where(padded > 0, ids, N_EXPERTS)
    later = jnp.min(jnp.where(ids[None, :] > ids[:, None], owners[None, :], N_EXPERTS), axis=1)
    next_expert = jnp.where(later < N_EXPERTS, later, -1).astype(jnp.int32)
    ys = moe_experts(hn, tok_of_slot, tile_expert, next_expert, n_active.reshape(1),
                     w_gate, w_up, w_down, tm)
    return moe_combine(x, gates, dest, ys, out_norm_g, final_norm)


def kernel(x, positions, l0_norm_mix, l0_w_in, l0_sinks, l0_conv_w, l0_conv_b, l0_ln_g, l0_ln_b, l0_w_out, l0_norm_ffn, l0_w_rg, l0_b_rg, l0_w_re, l0_b_re, l0_w_gate, l0_w_up, l0_w_down, l1_norm_mix, l1_w_in, l1_ln_g, l1_ln_b, l1_w_s, l1_b_s, l1_w_out, l1_norm_ffn, l1_w_rg, l1_b_rg, l1_w_re, l1_b_re, l1_w_gate, l1_w_up, l1_w_down, final_norm):
    b, s, d = x.shape
    t = b * s
    x = x.reshape(t, d)
    bf = lambda w: w.astype(BF16)

    proj = matmul([prenorm(x, l0_norm_mix)], [bf(l0_w_in)], out_dtype=BF16, name="l0_in_proj")
    attn = attention(proj, positions, l0_sinks, s)
    conv = conformer_conv(proj, l0_conv_w, l0_conv_b, l0_ln_g, l0_ln_b, s)
    w_out0 = bf(l0_w_out)
    x = matmul([attn, conv], [w_out0[:A_Q], w_out0[A_Q:]], res=x, out_dtype=F32, name="l0_out_proj")
    x, hn = hierarchical_moe(x, l0_norm_ffn, l0_w_rg, l0_b_rg, l0_w_re, l0_b_re,
                             l0_w_gate, l0_w_up, l0_w_down, l1_norm_mix, final_norm=False)
    z = matmul([hn], [bf(l1_w_in)], act="gelu", out_dtype=BF16, name="l1_in_proj")
    gated = spatial_gating(z, l1_ln_g, l1_ln_b, l1_w_s, l1_b_s)
    x = matmul([gated], [bf(l1_w_out)], res=x, out_dtype=F32, name="l1_out_proj")
    x = hierarchical_moe(x, l1_norm_ffn, l1_w_rg, l1_b_rg, l1_w_re, l1_b_re,
                         l1_w_gate, l1_w_up, l1_w_down, final_norm, final_norm=True)
    return x.reshape(b, s, d)
```

```python
import functools

import numpy as np
import jax
import jax.numpy as jnp
from jax import lax
from jax.experimental import pallas as pl
from jax.experimental.pallas import tpu as pltpu

D_MODEL = 4096
HEAD_DIM = 64
N_Q_HEADS = 32
N_KV_HEADS = 8
Q_PER_KV = N_Q_HEADS // N_KV_HEADS
BLOCK = 128
ROPE_THETA = 500000.0
ROPE_DIM = HEAD_DIM // 4
ROPE_HALF = ROPE_DIM // 2
A_Q = N_Q_HEADS * HEAD_DIM
A_KV = N_KV_HEADS * HEAD_DIM
CONV_CH = D_MODEL // 2
CONV_WIDTH = 31
CONV_HALO = 32
IN_EVEN = A_Q + 2 * A_KV + 2 * CONV_CH
CHUNK = 128
GMLP_DIM = D_MODEL
GMLP_GROUPS = 16
GMLP_GROUP_DIM = GMLP_DIM // GMLP_GROUPS
N_GROUPS = 4
EXPERTS_PER_GROUP = 8
N_EXPERTS = N_GROUPS * EXPERTS_PER_GROUP
D_EXPERT = 512
EPS = 1e-5
NEG_INF = -1e30
LOG2_E = float(np.log2(np.e))

V7X_LANES = 128
V7X_SUBLANES = 8
V7X_VMEM_BYTES = 64 * 1024 * 1024
VMEM_LIMIT = 56 * 1024 * 1024

F32 = jnp.float32
BF16 = jnp.bfloat16


def _tile(n, want):
    t = min(n, want)
    while n % t:
        t //= 2
    return t


def _params(*sem):
    return pltpu.CompilerParams(dimension_semantics=sem, vmem_limit_bytes=VMEM_LIMIT)


def _sigmoid(x):
    return 1.0 / (1.0 + jnp.exp(-x))


def _gelu_tanh(x):
    c = np.float32(np.sqrt(2.0 / np.pi))
    return x * (0.5 * (1.0 + jnp.tanh(c * (x + 0.044715 * (x * x * x)))))


def _rmsnorm_rows(x, g):
    ms = jnp.mean(x * x, axis=-1, keepdims=True)
    return x * lax.rsqrt(ms + EPS) * g


def _prenorm_kernel(x_ref, g_ref, o_ref):
    o_ref[...] = _rmsnorm_rows(x_ref[...], g_ref[...]).astype(o_ref.dtype)


def prenorm(x, g):
    t, d = x.shape
    tm = _tile(t, 512)
    return pl.pallas_call(
        _prenorm_kernel,
        out_shape=jax.ShapeDtypeStruct((t, d), BF16),
        grid=(t // tm,),
        in_specs=[pl.BlockSpec((tm, d), lambda i: (i, 0)), pl.BlockSpec((1, d), lambda i: (0, 0))],
        out_specs=pl.BlockSpec((tm, d), lambda i: (i, 0)),
        compiler_params=_params("parallel"),
        name="prenorm",
    )(x, g.reshape(1, d))


def _matmul_kernel(*refs, n_a, has_res, act):
    a_refs, w_refs = refs[:n_a], refs[n_a:2 * n_a]
    o_ref = refs[-1]
    acc = None
    for a_ref, w_ref in zip(a_refs, w_refs):
        part = jnp.dot(a_ref[...], w_ref[...], preferred_element_type=F32)
        acc = part if acc is None else acc + part
    if has_res:
        acc = acc + refs[2 * n_a][...]
    if act == "gelu":
        acc = _gelu_tanh(acc)
    o_ref[...] = acc.astype(o_ref.dtype)


def matmul(a_list, w_list, *, res=None, act=None, out_dtype, name):
    t = a_list[0].shape[0]
    n = w_list[0].shape[1]
    tm = _tile(t, 1024)
    tn = _tile(n, 1024)
    n_a = len(a_list)
    in_specs = [pl.BlockSpec((tm, a.shape[1]), lambda i, j: (i, 0)) for a in a_list]
    in_specs += [pl.BlockSpec((w.shape[0], tn), lambda i, j: (0, j)) for w in w_list]
    args = [*a_list, *w_list]
    if res is not None:
        in_specs.append(pl.BlockSpec((tm, tn), lambda i, j: (i, j)))
        args.append(res)
    return pl.pallas_call(
        functools.partial(_matmul_kernel, n_a=n_a, has_res=res is not None, act=act),
        out_shape=jax.ShapeDtypeStruct((t, n), out_dtype),
        grid=(t // tm, n // tn),
        in_specs=in_specs,
        out_specs=pl.BlockSpec((tm, tn), lambda i, j: (i, j)),
        compiler_params=_params("parallel", "arbitrary"),
        name=name,
    )(*args)


def _rope_tables(pos_ref, invf_ref):
    ang = pos_ref[...].astype(F32) * invf_ref[...]
    c = jnp.cos(ang)
    s = jnp.sin(ang)
    lane = lax.broadcasted_iota(jnp.int32, ang.shape, 1) % HEAD_DIM
    s_lo = jnp.where(lane < ROPE_HALF, -s, 0.0)
    s_hi = jnp.where((lane >= ROPE_HALF) & (lane < ROPE_DIM), s, 0.0)
    return c, s_lo, s_hi


def _apply_rope(x, tables):
    c, s_lo, s_hi = tables
    w = x.shape[1]
    reps = w // c.shape[1]
    c, s_lo, s_hi = (jnp.concatenate([t] * reps, axis=1) for t in (c, s_lo, s_hi))
    return (x * c + pltpu.roll(x, w - ROPE_HALF, 1) * s_lo + pltpu.roll(x, ROPE_HALF, 1) * s_hi)


def _attn_kernel(sink_ref, pos_c_ref, pos_p_ref, invf_ref, q_ref, kc_ref, kp_ref, vc_ref, vp_ref,
                 o_ref, *, blocks_per_seq):
    i = pl.program_id(0)
    not_first = (i % blocks_per_seq) > 0
    tab_c = _rope_tables(pos_c_ref, invf_ref)
    tab_p = _rope_tables(pos_p_ref, invf_ref)
    scale = HEAD_DIM ** -0.5 * LOG2_E
    q = _apply_rope(q_ref[...].astype(F32), tab_c) * scale
    k = jnp.concatenate([_apply_rope(kp_ref[...].astype(F32), tab_p),
                         _apply_rope(kc_ref[...].astype(F32), tab_c)], axis=0)
    v = jnp.concatenate([vp_ref[...], vc_ref[...]], axis=0).astype(F32)

    qi = lax.broadcasted_iota(jnp.int32, (Q_PER_KV * BLOCK, 2 * BLOCK), 0) % BLOCK + BLOCK
    kj = lax.broadcasted_iota(jnp.int32, (Q_PER_KV * BLOCK, 2 * BLOCK), 1)
    valid = (kj <= qi) & (qi - kj < BLOCK) & (not_first | (kj >= BLOCK))

    col = V7X_LANES
    low_q = lax.broadcasted_iota(jnp.int32, (BLOCK, col), 1) < HEAD_DIM
    low_kv = lax.broadcasted_iota(jnp.int32, (2 * BLOCK, col), 1) < HEAD_DIM
    ones = jnp.ones((2 * BLOCK, col), BF16)
    outs = []
    for c in range(A_KV // col):
        kcol = k[:, c * col:(c + 1) * col]
        vcol = v[:, c * col:(c + 1) * col]
        kswap = pltpu.roll(kcol, HEAD_DIM, 1)
        vswap = pltpu.roll(vcol, HEAD_DIM, 1)
        for half in range(col // HEAD_DIM):
            g = c * (col // HEAD_DIM) + half
            if half == 0:
                kk, vv = jnp.where(low_kv, kcol, kswap), jnp.where(low_kv, vcol, vswap)
            else:
                kk, vv = jnp.where(low_kv, kswap, kcol), jnp.where(low_kv, vswap, vcol)
            vv_ones = jnp.concatenate([vv.astype(BF16), ones], axis=1)
            qa = q[:, (2 * g) * col:(2 * g + 1) * col]
            qb = q[:, (2 * g + 1) * col:(2 * g + 2) * col]
            qs = jnp.concatenate([jnp.where(low_q, qa, 0.0), jnp.where(low_q, 0.0, qa),
                                  jnp.where(low_q, qb, 0.0), jnp.where(low_q, 0.0, qb)],
                                 axis=0).astype(BF16)
            s = lax.dot_general(qs, kk.astype(BF16), (((1,), (1,)), ((), ())),
                                preferred_element_type=F32)
            s = jnp.where(valid, s, NEG_INF)
            sink = jnp.concatenate(
                [jnp.full((BLOCK, 1), sink_ref[g * Q_PER_KV + h] * LOG2_E, F32)
                 for h in range(Q_PER_KV)], axis=0)
            m = jnp.maximum(jnp.max(s, axis=-1, keepdims=True), sink)
            p = jnp.exp2(s - m).astype(BF16)
            o = jnp.dot(p, vv_ones, preferred_element_type=F32)
            o = o[:, :col] / (o[:, col:] + jnp.exp2(sink - m))
            outs.append(jnp.where(low_q, o[0:BLOCK], o[BLOCK:2 * BLOCK]))
            outs.append(jnp.where(low_q, o[2 * BLOCK:3 * BLOCK], o[3 * BLOCK:4 * BLOCK]))
    o_ref[...] = jnp.concatenate(outs, axis=1).astype(o_ref.dtype)


def attention(proj, positions, sinks, seq_len):
    t = proj.shape[0]
    nb = t // BLOCK
    invf = ROPE_THETA ** (-np.arange(0, ROPE_DIM, 2, dtype=np.float32) / ROPE_DIM)
    lane = np.arange(V7X_LANES) % HEAD_DIM
    invf_row = np.where(lane < ROPE_DIM, invf[lane % ROPE_HALF], 0.0).astype(np.float32)
    pos = positions.reshape(t, 1)
    kcol = A_Q // A_KV
    prev = lambda i, s: (jnp.maximum(i - 1, 0), 0)
    return pl.pallas_call(
        functools.partial(_attn_kernel, blocks_per_seq=seq_len // BLOCK),
        out_shape=jax.ShapeDtypeStruct((t, A_Q), BF16),
        grid_spec=pltpu.PrefetchScalarGridSpec(
            num_scalar_prefetch=1,
            grid=(nb,),
            in_specs=[
                pl.BlockSpec((BLOCK, 1), lambda i, s: (i, 0)),
                pl.BlockSpec((BLOCK, 1), prev),
                pl.BlockSpec((1, V7X_LANES), lambda i, s: (0, 0)),
                pl.BlockSpec((BLOCK, A_Q), lambda i, s: (i, 0)),
                pl.BlockSpec((BLOCK, A_KV), lambda i, s: (i, kcol)),
                pl.BlockSpec((BLOCK, A_KV), lambda i, s: (jnp.maximum(i - 1, 0), kcol)),
                pl.BlockSpec((BLOCK, A_KV), lambda i, s: (i, kcol + 1)),
                pl.BlockSpec((BLOCK, A_KV), lambda i, s: (jnp.maximum(i - 1, 0), kcol + 1)),
            ],
            out_specs=pl.BlockSpec((BLOCK, A_Q), lambda i, s: (i, 0)),
        ),
        compiler_params=_params("parallel"),
        name="swa_attention",
    )(sinks.astype(F32), pos, pos, jnp.asarray(invf_row).reshape(1, V7X_LANES),
      proj, proj, proj, proj, proj)


def _conv_kernel(a0_ref, a1_ref, g0_ref, g1_ref, w_ref, cb_ref, lg_ref, lb_ref, o_ref,
                 hbuf, ybuf, hsh, *, steps_per_seq):
    i = pl.program_id(0)
    ts = o_ref.shape[0]
    half = CONV_CH // 2

    @pl.when(i % steps_per_seq == 0)
    def _():
        hbuf[0:CONV_HALO, :] = jnp.zeros((CONV_HALO, CONV_CH), F32)

    @pl.when(i % steps_per_seq != 0)
    def _():
        hbuf[0:CONV_HALO, :] = hbuf[ts:ts + CONV_HALO, :]

    for c, (a_ref, g_ref) in enumerate(((a0_ref, g0_ref), (a1_ref, g1_ref))):
        hbuf[CONV_HALO:CONV_HALO + ts, c * half:(c + 1) * half] = (
            a_ref[...].astype(F32) * _sigmoid(g_ref[...].astype(F32)))

    sub = V7X_SUBLANES
    first_tap = CONV_HALO - (CONV_WIDTH - 1)
    h_all = hbuf[...]
    for s in range(1, sub):
        hsh[s - 1] = pltpu.roll(h_all, ts + CONV_HALO - s, 0)

    rows = 2 * sub
    lanes = CONV_CH // 4

    def body(r, carry):
        base = pl.multiple_of(r * rows, rows)
        for c in range(CONV_CH // lanes):
            csl = slice(c * lanes, (c + 1) * lanes)
            acc = [jnp.broadcast_to(cb_ref[:, csl], (sub, lanes)) for _ in range(rows // sub)]
            for j in range(CONV_WIDTH):
                off = first_tap + j
                w = w_ref[j * sub:(j + 1) * sub, csl]
                for b in range(rows // sub):
                    start = base + (off // sub + b) * sub
                    if off % sub == 0:
                        tap = hbuf[pl.ds(start, sub), csl]
                    else:
                        tap = hsh[off % sub - 1, pl.ds(start, sub), csl]
                    acc[b] = acc[b] + tap * w
            for b in range(rows // sub):
                ybuf[pl.ds(base + b * sub, sub), csl] = acc[b]
        return carry

    lax.fori_loop(0, ts // rows, body, 0)

    y = ybuf[...]
    mu = jnp.mean(y, axis=-1, keepdims=True)
    yc = y - mu
    var = jnp.mean(yc * yc, axis=-1, keepdims=True)
    z = yc * lax.rsqrt(var + EPS) * lg_ref[...] + lb_ref[...]
    o_ref[...] = (z * _sigmoid(z)).astype(o_ref.dtype)


def conformer_conv(proj, conv_w, conv_b, ln_g, ln_b, seq_len):
    t = proj.shape[0]
    ts = _tile(seq_len, 256)
    half = CONV_CH // 2
    col0 = (A_Q + 2 * A_KV) // half
    row = lambda v: v.reshape(1, CONV_CH).astype(F32)
    full = lambda shape: pl.BlockSpec(shape, lambda i: (0, 0))
    return pl.pallas_call(
        functools.partial(_conv_kernel, steps_per_seq=seq_len // ts),
        out_shape=jax.ShapeDtypeStruct((t, CONV_CH), BF16),
        grid=(t // ts,),
        in_specs=[
            pl.BlockSpec((ts, half), lambda i: (i, col0)),
            pl.BlockSpec((ts, half), lambda i: (i, col0 + 1)),
            pl.BlockSpec((ts, half), lambda i: (i, col0 + 2)),
            pl.BlockSpec((ts, half), lambda i: (i, col0 + 3)),
            full((CONV_WIDTH * V7X_SUBLANES, CONV_CH)),
            full((1, CONV_CH)), full((1, CONV_CH)), full((1, CONV_CH)),
        ],
        out_specs=pl.BlockSpec((ts, CONV_CH), lambda i: (i, 0)),
        scratch_shapes=[pltpu.VMEM((ts + CONV_HALO, CONV_CH), F32), pltpu.VMEM((ts, CONV_CH), F32),
                        pltpu.VMEM((V7X_SUBLANES - 1, ts + CONV_HALO, CONV_CH), F32)],
        compiler_params=_params("arbitrary"),
        name="conformer_conv",
    )(proj, proj, proj, proj, jnp.repeat(conv_w.astype(F32), V7X_SUBLANES, axis=0),
      row(conv_b), row(ln_g), row(ln_b))


def _sgu_kernel(u_ref, v_ref, lg_ref, lb_ref, ws_ref, bs_ref, o_ref):
    rows = o_ref.shape[0]
    v = v_ref[...].astype(F32)
    mu = jnp.mean(v, axis=-1, keepdims=True)
    vc = v - mu
    var = jnp.mean(vc * vc, axis=-1, keepdims=True)
    vn = (vc * lax.rsqrt(var + EPS) * lg_ref[...] + lb_ref[...]).astype(BF16)
    ti = lax.broadcasted_iota(jnp.int32, (CHUNK, CHUNK), 0)
    si = lax.broadcasted_iota(jnp.int32, (CHUNK, CHUNK), 1)
    causal = si <= ti
    for c in range(rows // CHUNK):
        rsl = slice(c * CHUNK, (c + 1) * CHUNK)
        for g in range(GMLP_GROUPS):
            csl = slice(g * GMLP_GROUP_DIM, (g + 1) * GMLP_GROUP_DIM)
            ws = jnp.where(causal, ws_ref[g], 0.0).astype(BF16)
            sv = jnp.dot(ws, vn[rsl, csl], preferred_element_type=F32) + bs_ref[:, g:g + 1]
            o_ref[rsl, csl] = (u_ref[rsl, csl].astype(F32) * sv).astype(o_ref.dtype)


def spatial_gating(z, ln_g, ln_b, w_s, b_s):
    t = z.shape[0]
    rows = _tile(t, 2 * CHUNK)
    row = lambda v: v.reshape(1, GMLP_DIM).astype(F32)
    return pl.pallas_call(
        _sgu_kernel,
        out_shape=jax.ShapeDtypeStruct((t, GMLP_DIM), BF16),
        grid=(t // rows,),
        in_specs=[
            pl.BlockSpec((rows, GMLP_DIM), lambda i: (i, 0)),
            pl.BlockSpec((rows, GMLP_DIM), lambda i: (i, 1)),
            pl.BlockSpec((1, GMLP_DIM), lambda i: (0, 0)),
            pl.BlockSpec((1, GMLP_DIM), lambda i: (0, 0)),
            pl.BlockSpec((GMLP_GROUPS, CHUNK, CHUNK), lambda i: (0, 0, 0)),
            pl.BlockSpec((CHUNK, GMLP_GROUPS), lambda i: (0, 0)),
        ],
        out_specs=pl.BlockSpec((rows, GMLP_DIM), lambda i: (i, 0)),
        compiler_params=_params("parallel"),
        name="spatial_gating",
    )(z, z, row(ln_g), row(ln_b), w_s.astype(F32), b_s.T.astype(F32))


ROUTER_LANES = V7X_LANES


def _first_argmax(x, lane, width):
    m = jnp.max(x, axis=-1, keepdims=True)
    idx = jnp.min(jnp.where(x == m, lane, width), axis=-1, keepdims=True)
    return m, idx


def _pack_bf16_pairs(x):
    c = x.shape[1] // 2
    lo = pltpu.bitcast(x[:, :c].astype(BF16).astype(F32), jnp.int32)
    hi = pltpu.bitcast(x[:, c:].astype(BF16).astype(F32), jnp.int32)
    return hi | lax.shift_right_logical(lo, 16)


def _unpack_bf16_pairs(p):
    lo = pltpu.bitcast(lax.shift_left(p, 16), F32)
    hi = pltpu.bitcast(p & jnp.int32(-65536), F32)
    return jnp.concatenate([lo, hi], axis=1)


ROW_SLAB = D_MODEL // 2 // V7X_LANES


def _store_row_slabs(ref, packed):
    rows = packed.shape[0]
    for s in range(ROW_SLAB):
        ref[pl.ds(s, rows, stride=ROW_SLAB), :] = packed[:, s * V7X_LANES:(s + 1) * V7X_LANES]


def _load_row_slabs(ref, rows):
    return jnp.concatenate([ref[pl.ds(s, rows, stride=ROW_SLAB), :] for s in range(ROW_SLAB)], axis=1)


def _router_kernel(x_ref, g_ref, whi_ref, wlo_ref, b_ref, hn_ref, eid_ref, gate_ref, rank_ref, cnt_ref,
                   carry):
    i = pl.program_id(0)
    tm = x_ref.shape[0]

    @pl.when(i == 0)
    def _():
        carry[...] = jnp.zeros_like(carry)

    h = _rmsnorm_rows(x_ref[...], g_ref[...])
    _store_row_slabs(hn_ref, _pack_bf16_pairs(h))
    h_hi = h.astype(BF16)
    h_lo = (h - h_hi.astype(F32)).astype(BF16)
    logits = (jnp.dot(h_hi, whi_ref[...], preferred_element_type=F32)
              + jnp.dot(h_lo, whi_ref[...], preferred_element_type=F32)
              + jnp.dot(h_hi, wlo_ref[...], preferred_element_type=F32) + b_ref[...])
    lane = lax.broadcasted_iota(jnp.int32, logits.shape, 1)
    gl = jnp.where(lane < N_GROUPS, logits, NEG_INF)
    gmax, g_idx = _first_argmax(gl, lane, ROUTER_LANES)
    g_gate = 1.0 / jnp.sum(jnp.exp(gl - gmax), axis=-1, keepdims=True)
    e_lane = lane - N_GROUPS
    in_group = (e_lane >= g_idx * EXPERTS_PER_GROUP) & (e_lane < (g_idx + 1) * EXPERTS_PER_GROUP)
    el = jnp.where(in_group, logits, NEG_INF)
    l1, i1 = _first_argmax(el, lane, ROUTER_LANES)
    el2 = jnp.where(lane == i1, NEG_INF, el)
    l2, i2 = _first_argmax(el2, lane, ROUTER_LANES)
    r = jnp.exp(l2 - l1)
    w1 = g_gate / (1.0 + r)
    w2 = g_gate * r / (1.0 + r)
    e1 = i1 - N_GROUPS
    e2 = i2 - N_GROUPS
    hot1 = lane == e1
    hot2 = lane == e2
    hot = (hot1 | hot2).astype(BF16)
    ri = lax.broadcasted_iota(jnp.int32, (tm, tm), 0)
    ci = lax.broadcasted_iota(jnp.int32, (tm, tm), 1)
    before = (ci < ri).astype(BF16)
    seen = jnp.dot(before, hot, preferred_element_type=F32) + carry[...]
    r1 = jnp.sum(jnp.where(hot1, seen, 0.0), axis=-1, keepdims=True)
    r2 = jnp.sum(jnp.where(hot2, seen, 0.0), axis=-1, keepdims=True)
    carry[...] = carry[...] + jnp.sum(hot.astype(F32), axis=0, keepdims=True)
    eid_ref[...] = jnp.concatenate([e1, e2], axis=1)
    gate_ref[...] = jnp.concatenate([w1, w2], axis=1)
    rank_ref[...] = jnp.concatenate([r1, r2], axis=1).astype(jnp.int32)
    cnt_ref[...] = carry[...].astype(jnp.int32)


def moe_router(x, norm, w_rg, b_rg, w_re, b_re):
    t, d = x.shape
    tm = _tile(t, 256)
    pad = ROUTER_LANES - N_GROUPS - N_EXPERTS
    w = jnp.concatenate([w_rg, w_re, jnp.zeros((d, pad), F32)], axis=1)
    w_hi = w.astype(BF16)
    w_lo = (w - w_hi.astype(F32)).astype(BF16)
    b = jnp.concatenate([b_rg, b_re, jnp.zeros((pad,), F32)]).reshape(1, ROUTER_LANES)
    pair = lambda dt: jax.ShapeDtypeStruct((t, 2), dt)
    pair_spec = pl.BlockSpec((tm, 2), lambda i: (i, 0))
    w_spec = pl.BlockSpec((d, ROUTER_LANES), lambda i: (0, 0))
    return pl.pallas_call(
        _router_kernel,
        out_shape=(jax.ShapeDtypeStruct((t * ROW_SLAB, V7X_LANES), jnp.int32), pair(jnp.int32), pair(F32),
                   pair(jnp.int32), jax.ShapeDtypeStruct((1, ROUTER_LANES), jnp.int32)),
        grid=(t // tm,),
        in_specs=[
            pl.BlockSpec((tm, d), lambda i: (i, 0)),
            pl.BlockSpec((1, d), lambda i: (0, 0)),
            w_spec, w_spec,
            pl.BlockSpec((1, ROUTER_LANES), lambda i: (0, 0)),
        ],
        out_specs=(pl.BlockSpec((tm * ROW_SLAB, V7X_LANES), lambda i: (i, 0)), pair_spec, pair_spec,
                   pair_spec,
                   pl.BlockSpec((1, ROUTER_LANES), lambda i: (0, 0))),
        scratch_shapes=[pltpu.VMEM((1, ROUTER_LANES), F32)],
        compiler_params=_params("arbitrary"),
        name="moe_router",
    )(x, norm.reshape(1, d), w_hi, w_lo, b)


def _row_copy(src_hbm, dst, src_row, dst_row, sem):
    return pltpu.make_async_copy(src_hbm.at[pl.ds(src_row * ROW_SLAB, ROW_SLAB)],
                                 dst.at[pl.ds(dst_row * ROW_SLAB, ROW_SLAB)], sem)


def _experts_kernel(te_ref, nxt_ref, na_ref, tok_ref, hn_hbm, wg_hbm, wu_hbm, wd_hbm, ys_ref,
                    hbuf, stage_g, stage_u, stage_d, wg_b, wu_b, wd_b, gsem, wsem):
    i = pl.program_id(0)
    tm = ys_ref.shape[0] // ROW_SLAB
    n_active = na_ref[0]
    e = te_ref[i]
    run_start = (i == 0) | (e != te_ref[jnp.maximum(i - 1, 0)])

    def weight_copies(expert):
        return (pltpu.make_async_copy(wg_hbm.at[expert], stage_g, wsem.at[0]),
                pltpu.make_async_copy(wu_hbm.at[expert], stage_u, wsem.at[1]),
                pltpu.make_async_copy(wd_hbm.at[expert], stage_d, wsem.at[2]))

    def gather(tile, slot):
        def body(r, carry):
            _row_copy(hn_hbm, hbuf.at[slot], tok_ref[tile * tm + r], r, gsem.at[slot]).start()
            return carry
        lax.fori_loop(0, tm, body, 0, unroll=8)

    @pl.when((i == 0) & (n_active > 0))
    def _():
        for c in weight_copies(e):
            c.start(priority=1)
        gather(0, 0)

    @pl.when((i < n_active) & run_start)
    def _():
        for c in weight_copies(e):
            c.wait()
        wg_b[...] = stage_g[...].astype(BF16)
        wu_b[...] = stage_u[...].astype(BF16)
        wd_b[...] = stage_d[...].astype(BF16)
        nxt = nxt_ref[e]

        @pl.when(nxt >= 0)
        def _():
            for c in weight_copies(nxt):
                c.start(priority=1)

    def wait_gather(slot):
        pltpu.make_async_copy(hn_hbm.at[pl.ds(0, tm * ROW_SLAB)], hbuf.at[slot], gsem.at[slot]).wait()

    @pl.when(i < n_active)
    def _():
        slot = i % 2
        wait_gather(slot)
        h = _unpack_bf16_pairs(_load_row_slabs(hbuf.at[slot], tm)).astype(BF16)
        for r in range(tm):
            _row_copy(hn_hbm, hbuf.at[1 - slot], tok_ref[(i + 1) * tm + r], r, gsem.at[1 - slot]).start()
        g = jnp.dot(h, wg_b[...], preferred_element_type=F32)
        u = jnp.dot(h, wu_b[...], preferred_element_type=F32)
        act = (g * _sigmoid(g) * u).astype(BF16)
        _store_row_slabs(ys_ref, _pack_bf16_pairs(jnp.dot(act, wd_b[...], preferred_element_type=F32)))

    @pl.when((i == n_active) & (i > 0))
    def _():
        wait_gather(i % 2)

    @pl.when(i >= n_active)
    def _():
        ys_ref[...] = jnp.zeros_like(ys_ref)


def moe_experts(hn, tok_of_slot, tile_expert, next_expert, n_active, w_gate, w_up, w_down, tm):
    d = w_gate.shape[1]
    n_slots = tok_of_slot.shape[0]
    any_spec = pl.BlockSpec(memory_space=pl.ANY)
    return pl.pallas_call(
        _experts_kernel,
        out_shape=jax.ShapeDtypeStruct((n_slots * ROW_SLAB, V7X_LANES), jnp.int32),
        grid_spec=pltpu.PrefetchScalarGridSpec(
            num_scalar_prefetch=4,
            grid=(n_slots // tm,),
            in_specs=[any_spec, any_spec, any_spec, any_spec],
            out_specs=pl.BlockSpec((tm * ROW_SLAB, V7X_LANES), lambda i, *_: (i, 0)),
            scratch_shapes=[
                pltpu.VMEM((2, tm * ROW_SLAB, V7X_LANES), jnp.int32),
                pltpu.VMEM((d, D_EXPERT), F32), pltpu.VMEM((d, D_EXPERT), F32),
                pltpu.VMEM((D_EXPERT, d), F32),
                pltpu.VMEM((d, D_EXPERT), BF16), pltpu.VMEM((d, D_EXPERT), BF16),
                pltpu.VMEM((D_EXPERT, d), BF16),
                pltpu.SemaphoreType.DMA((2,)), pltpu.SemaphoreType.DMA((3,)),
            ],
        ),
        compiler_params=_params("arbitrary"),
        name="moe_experts",
    )(tile_expert, next_expert, n_active, tok_of_slot, hn, w_gate, w_up, w_down)


def _combine_kernel(dest_ref, x_ref, gate_ref, ys_hbm, ng_ref, *rest, final_norm):
    if final_norm:
        o_ref, ybuf, sems = rest
    else:
        o_ref, hn_ref, ybuf, sems = rest
    i = pl.program_id(0)
    tc = x_ref.shape[0]

    last = pl.num_programs(0) - 1

    def start_row(step, slot, k):
        tok = step * tc + k
        _row_copy(ys_hbm, ybuf.at[slot, 0], dest_ref[2 * tok], k, sems.at[slot]).start(priority=0)
        _row_copy(ys_hbm, ybuf.at[slot, 1], dest_ref[2 * tok + 1], k, sems.at[slot]).start(priority=1)

    def wait_rows(slot):
        for c in range(2):
            pltpu.make_async_copy(ys_hbm.at[pl.ds(0, tc * ROW_SLAB)], ybuf.at[slot, c],
                                  sems.at[slot]).wait()

    @pl.when(i == 0)
    def _():
        lax.fori_loop(0, tc, lambda k, c: (start_row(0, 0, k), c)[1], 0, unroll=4)

    slot = i % 2
    wait_rows(slot)
    for k in range(tc):
        start_row(jnp.minimum(i + 1, last), 1 - slot, k)
    gate = gate_ref[...]
    y = (x_ref[...] + gate[:, 0:1] * _unpack_bf16_pairs(_load_row_slabs(ybuf.at[slot, 0], tc))
         + gate[:, 1:2] * _unpack_bf16_pairs(_load_row_slabs(ybuf.at[slot, 1], tc)))
    if final_norm:
        o_ref[...] = _rmsnorm_rows(y, ng_ref[...])
    else:
        o_ref[...] = y
        hn_ref[...] = _rmsnorm_rows(y, ng_ref[...]).astype(hn_ref.dtype)

    @pl.when(i == last)
    def _():
        wait_rows(1 - slot)


def moe_combine(x, gates, dest, ys, norm_g, final_norm):
    t, d = x.shape
    tc = _tile(t, 256)
    row_spec = pl.BlockSpec((tc, d), lambda i, dst: (i, 0))
    out_shape = jax.ShapeDtypeStruct((t, d), F32)
    out_specs = row_spec
    if not final_norm:
        out_shape = (out_shape, jax.ShapeDtypeStruct((t, d), BF16))
        out_specs = (row_spec, row_spec)
    return pl.pallas_call(
        functools.partial(_combine_kernel, final_norm=final_norm),
        out_shape=out_shape,
        grid_spec=pltpu.PrefetchScalarGridSpec(
            num_scalar_prefetch=1,
            grid=(t // tc,),
            in_specs=[
                row_spec,
                pl.BlockSpec((tc, 2), lambda i, dst: (i, 0)),
                pl.BlockSpec(memory_space=pl.ANY),
                pl.BlockSpec((1, d), lambda i, dst: (0, 0)),
            ],
            out_specs=out_specs,
            scratch_shapes=[pltpu.VMEM((2, 2, tc * ROW_SLAB, V7X_LANES), jnp.int32),
                            pltpu.SemaphoreType.DMA((2,))],
        ),
        compiler_params=_params("arbitrary"),
        name="moe_combine",
    )(dest, x, gates, ys, norm_g.reshape(1, d))


def hierarchical_moe(x, norm, w_rg, b_rg, w_re, b_re, w_gate, w_up, w_down, out_norm_g, final_norm):
    t, d = x.shape
    tm = _tile(t, 256)
    n_slots = 2 * t + N_EXPERTS * tm
    hn, eid, gates, rank, counts = moe_router(x, norm, w_rg, b_rg, w_re, b_re)
    counts = counts[0, :N_EXPERTS]
    padded = (counts + tm - 1) // tm * tm
    ends = jnp.cumsum(padded)
    dest = ((ends - padded)[eid] + rank).reshape(2 * t).astype(jnp.int32)
    n_active = (ends[-1] // tm).astype(jnp.int32)
    tile_row = jnp.minimum(jnp.arange(n_slots // tm, dtype=jnp.int32), n_active - 1) * tm
    tile_expert = jnp.sum(tile_row[:, None] >= ends[None, :], axis=1).astype(jnp.int32)
    tok_of_slot = jnp.zeros((n_slots,), jnp.int32).at[dest].set(
        jnp.arange(2 * t, dtype=jnp.int32) // 2, unique_indices=True)
    ids = jnp.arange(N_EXPERTS, dtype=jnp.int32)
    owners = jnp.where(padded > 0, ids, N_EXPERTS)
    later = jnp.min(jnp.where(ids[None, :] > ids[:, None], owners[None, :], N_EXPERTS), axis=1)
    next_expert = jnp.where(later < N_EXPERTS, later, -1).astype(jnp.int32)
    ys = moe_experts(hn, tok_of_slot, tile_expert, next_expert, n_active.reshape(1),
                     w_gate, w_up, w_down, tm)
    return moe_combine(x, gates, dest, ys, out_norm_g, final_norm)


def kernel(x, positions, l0_norm_mix, l0_w_in, l0_sinks, l0_conv_w, l0_conv_b, l0_ln_g, l0_ln_b, l0_w_out, l0_norm_ffn, l0_w_rg, l0_b_rg, l0_w_re, l0_b_re, l0_w_gate, l0_w_up, l0_w_down, l1_norm_mix, l1_w_in, l1_ln_g, l1_ln_b, l1_w_s, l1_b_s, l1_w_out, l1_norm_ffn, l1_w_rg, l1_b_rg, l1_w_re, l1_b_re, l1_w_gate, l1_w_up, l1_w_down, final_norm):
    b, s, d = x.shape
    t = b * s
    x = x.reshape(t, d)
    bf = lambda w: w.astype(BF16)

    proj = matmul([prenorm(x, l0_norm_mix)], [bf(l0_w_in)], out_dtype=BF16, name="l0_in_proj")
    attn = attention(proj, positions, l0_sinks, s)
    conv = conformer_conv(proj, l0_conv_w, l0_conv_b, l0_ln_g, l0_ln_b, s)
    w_out0 = bf(l0_w_out)
    x = matmul([attn, conv], [w_out0[:A_Q], w_out0[A_Q:]], res=x, out_dtype=F32, name="l0_out_proj")
    x, hn = hierarchical_moe(x, l0_norm_ffn, l0_w_rg, l0_b_rg, l0_w_re, l0_b_re,
                             l0_w_gate, l0_w_up, l0_w_down, l1_norm_mix, final_norm=False)
    z = matmul([hn], [bf(l1_w_in)], act="gelu", out_dtype=BF16, name="l1_in_proj")
    gated = spatial_gating(z, l1_ln_g, l1_ln_b, l1_w_s, l1_b_s)
    x = matmul([gated], [bf(l1_w_out)], res=x, out_dtype=F32, name="l1_out_proj")
    x = hierarchical_moe(x, l1_norm_ffn, l1_w_rg, l1_b_rg, l1_w_re, l1_b_re,
                         l1_w_gate, l1_w_up, l1_w_down, final_norm, final_norm=True)
    return x.reshape(b, s, d)
```
